```python
import math
import jax, jax.numpy as jnp
from jax import lax
import numpy as np

D_MODEL = 1024
BATCH = 32
SEQ = 2048
DEPTH = 4

CTX_LEN = 256
GRID_W = 64
EPS = 1e-6
ROPE_BASE = 10000.0
N_EVEN = (DEPTH + 1) // 2
N_ODD = DEPTH // 2

GDN_HEADS = 4
GDN_DK = 128
GDN_DV = 128
GDN_CHUNK = 64
SHORT_CONV = 5

MLA_HEADS = 4
MLA_NOPE = 128
MLA_ROPE = 64
MLA_V = 128
MLA_Q_RANK = 256
MLA_KV_RANK = 256
Q_BLOCK = 128

NA_HEADS = 16
NA_DH = 64
NA_KH = 8
NA_KW = 16

D_FF = 2816
FFN_CONV = 3

GDN_QKV = 2 * GDN_HEADS * GDN_DK + GDN_HEADS * GDN_DV
GDN_VW = GDN_HEADS * GDN_DV
EVEN_SPLITS = (GDN_QKV, GDN_VW, 2 * GDN_HEADS, 2 * GDN_HEADS, MLA_Q_RANK, MLA_KV_RANK, MLA_ROPE)
EVEN_IN = GDN_QKV + GDN_VW + 4 * GDN_HEADS + MLA_Q_RANK + MLA_KV_RANK + MLA_ROPE
EVEN_MIX = GDN_VW + MLA_HEADS * MLA_V
NA_W = NA_HEADS * NA_DH

kernel_name = 'hybrid_gdn_mla_natten_dit'


def rmsnorm(x, g):
    xf = x.astype(jnp.float32)
    y = xf * lax.rsqrt(jnp.mean(xf * xf, axis=-1, keepdims=True) + EPS)
    return (y * g.astype(jnp.float32)).astype(x.dtype)


def l2norm(x):
    xf = x.astype(jnp.float32)
    return (xf * lax.rsqrt(jnp.sum(xf * xf, axis=-1, keepdims=True) + EPS)).astype(x.dtype)


def modulate(h, shift, scale):
    return h * (1 + scale) + shift


def ada_chunks(cond, w, b):
    m = jax.nn.silu(cond) @ w + b
    return jnp.split(m[..., None, :], 6, axis=-1)


def dwconv_seq(x, w):
    k = w.shape[0]
    return lax.conv_general_dilated(
        x, w[:, None, :].astype(x.dtype), window_strides=(1,), padding=[(k // 2, k // 2)],
        dimension_numbers=('NWC', 'WIO', 'NWC'), feature_group_count=x.shape[-1])


def axial_rope_tables(n_tokens, dim):
    t = jnp.arange(n_tokens)
    row = (t // GRID_W).astype(jnp.float32)
    col = (t % GRID_W).astype(jnp.float32)
    quarter = dim // 4
    inv_freq = ROPE_BASE ** (-jnp.arange(quarter, dtype=jnp.float32) / quarter)
    ang_r = row[:, None] * inv_freq
    ang_c = col[:, None] * inv_freq
    ang = jnp.concatenate([ang_r, ang_r, ang_c, ang_c], axis=-1)
    return jnp.cos(ang), jnp.sin(ang)


def rotate_half(u):
    a, b = jnp.split(u, 2, axis=-1)
    return jnp.concatenate([-b, a], axis=-1)


def apply_axial_rope(x, cos, sin):
    half = x.shape[-1] // 2
    rot = jnp.concatenate([rotate_half(x[..., :half]), rotate_half(x[..., half:])], axis=-1)
    return (x * cos[:, None, :] + rot * sin[:, None, :]).astype(x.dtype)


def blocked_attention(q, k, v, scale):
    b, lq, h, d = q.shape
    nb = lq // Q_BLOCK
    q_blocks = jnp.moveaxis(q.reshape(b, nb, Q_BLOCK, h, d), 1, 0)

    def one_block(qb):
        s = jnp.einsum('bqhd,bkhd->bhqk', qb, k).astype(jnp.float32) * scale
        p = jax.nn.softmax(s, axis=-1).astype(v.dtype)
        return jnp.einsum('bhqk,bkhd->bqhd', p, v)

    o = lax.map(one_block, q_blocks)
    return jnp.moveaxis(o, 0, 1).reshape(b, lq, h * v.shape[-1])


def split_even(p):
    at = [int(s) for s in np.cumsum(EVEN_SPLITS)[:-1]]
    return jnp.split(p, at, axis=-1)


def gdn_inputs(qkv, b_raw, a_raw, conv_w, a_log, dt_bias):
    b, n, _ = qkv.shape
    h = GDN_HEADS
    qkv = jax.nn.silu(dwconv_seq(qkv, conv_w))
    q, k, v = jnp.split(qkv, [h * GDN_DK, 2 * h * GDN_DK], axis=-1)
    to_heads = lambda t: jnp.moveaxis(t.reshape(b, n, h, -1), 2, 1)
    q = l2norm(to_heads(q)) * (GDN_DK ** -0.5)
    k = l2norm(to_heads(k))
    v = to_heads(v)
    dir_heads = lambda t: t.astype(jnp.float32).reshape(b, n, 2, h).transpose(2, 0, 3, 1)
    beta = jax.nn.sigmoid(dir_heads(b_raw))
    g = -jnp.exp(a_log.astype(jnp.float32))[:, None, :, None] * jax.nn.softplus(
        dir_heads(a_raw) + dt_bias.astype(jnp.float32)[:, None, :, None])
    return q, k, v, beta, g


def gdn_chunked(q, k, v, beta, g, s0):
    b, h, n, dk = q.shape
    dv = v.shape[-1]
    c = GDN_CHUNK
    nc = n // c
    q, k, v = [t.astype(jnp.float32).reshape(b, h, nc, c, t.shape[-1]) for t in (q, k, v)]
    beta = beta.reshape(b, h, nc, c)
    gcum = jnp.cumsum(g.reshape(b, h, nc, c), axis=-1)
    diff = gcum[..., :, None] - gcum[..., None, :]
    incl = jnp.tri(c, dtype=bool)
    strict = jnp.tri(c, k=-1, dtype=bool)
    decay = jnp.where(incl, jnp.exp(jnp.where(incl, diff, 0.0)), 0.0)
    kk = jnp.einsum('bhncd,bhned->bhnce', k, k)
    a_mat = jnp.where(strict, beta[..., None] * kk * decay, 0.0)
    rhs = jnp.concatenate([k * (beta * jnp.exp(gcum))[..., None], v * beta[..., None]], axis=-1)
    sol = lax.linalg.triangular_solve(jnp.eye(c, dtype=jnp.float32) + a_mat, rhs,
                                      left_side=True, lower=True, unit_diagonal=True)
    w_mat, u_tilde = sol[..., :dk], sol[..., dk:]
    qk = jnp.einsum('bhncd,bhned->bhnce', q, k) * decay
    q_dec = q * jnp.exp(gcum)[..., None]
    k_dec = k * jnp.exp(gcum[..., -1:] - gcum)[..., None]
    g_last = jnp.exp(gcum[..., -1])

    def step(s, xs):
        w_c, u_c, qk_c, qd_c, kd_c, gl_c = xs
        u = u_c - jnp.einsum('bhcd,bhde->bhce', w_c, s)
        o = jnp.einsum('bhcd,bhde->bhce', qd_c, s) + jnp.einsum('bhce,bhef->bhcf', qk_c, u)
        s = gl_c[..., None, None] * s + jnp.einsum('bhcd,bhce->bhde', kd_c, u)
        return s, o

    xs = tuple(jnp.moveaxis(t, 2, 0) for t in (w_mat, u_tilde, qk, q_dec, k_dec, g_last))
    s_final, o = lax.scan(step, s0, xs)
    return jnp.moveaxis(o, 0, 2).reshape(b, h, n, dv), s_final


def bidir_gdn(lat, ctx):
    ql, kl, vl, bl, gl = lat
    qc, kc, vc, bc, gc = ctx
    s0 = jnp.zeros((ql.shape[0], GDN_HEADS, GDN_DK, GDN_DV), jnp.float32)
    fl = lambda t: jnp.flip(t, axis=2)
    oc_f, sc_f = gdn_chunked(qc, kc, vc, bc[0], gc[0], s0)
    ol_f, _ = gdn_chunked(ql, kl, vl, bl[0], gl[0], sc_f)
    oc_b, sc_b = gdn_chunked(fl(qc), fl(kc), fl(vc), fl(bc[1]), fl(gc[1]), s0)
    ol_b, _ = gdn_chunked(fl(ql), fl(kl), fl(vl), fl(bl[1]), fl(gl[1]), sc_b)
    return ol_f + fl(ol_b), oc_f + fl(oc_b)


def gdn_output(o, z, gain):
    b, h, n, dv = o.shape
    o = jnp.moveaxis(o, 1, 2).astype(z.dtype)
    o = rmsnorm(o, gain) * jax.nn.silu(z.reshape(b, n, h, dv))
    return o.reshape(b, n, h * dv)


def mla_q(cq, q_norm, w_uq, rope):
    b, n, _ = cq.shape
    q = (rmsnorm(cq, q_norm) @ w_uq).reshape(b, n, MLA_HEADS, MLA_NOPE + MLA_ROPE)
    if rope is None:
        return q
    return jnp.concatenate([q[..., :MLA_NOPE], apply_axial_rope(q[..., MLA_NOPE:], *rope)], axis=-1)


def mla_kv(ckv, k_rope, kv_norm, w_ukv, rope):
    b, n, _ = ckv.shape
    kv = (rmsnorm(ckv, kv_norm) @ w_ukv).reshape(b, n, MLA_HEADS, MLA_NOPE + MLA_V)
    k_rope = k_rope[:, :, None, :]
    if rope is not None:
        k_rope = apply_axial_rope(k_rope, *rope)
    k = jnp.concatenate([kv[..., :MLA_NOPE], jnp.broadcast_to(k_rope, (b, n, MLA_HEADS, MLA_ROPE))], axis=-1)
    return k, kv[..., MLA_NOPE:]


def even_mixer(h_lat, h_ctx, w_in, conv_w, a_log, dt_bias, gdn_g, q_norm, kv_norm, w_uq, w_ukv, w_out,
               rope, ctx_out):
    pl = split_even(h_lat @ w_in)
    pc = split_even(h_ctx @ w_in)
    o_lat_a, o_ctx_a = bidir_gdn(gdn_inputs(pl[0], pl[2], pl[3], conv_w, a_log, dt_bias),
                                 gdn_inputs(pc[0], pc[2], pc[3], conv_w, a_log, dt_bias))
    a_lat = gdn_output(o_lat_a, pl[1], gdn_g)
    scale = (MLA_NOPE + MLA_ROPE) ** -0.5
    ql = mla_q(pl[4], q_norm, w_uq, rope)
    kl, vl = mla_kv(pl[5], pl[6], kv_norm, w_ukv, rope)
    kc, vc = mla_kv(pc[5], pc[6], kv_norm, w_ukv, None)
    b_lat = blocked_attention(ql, jnp.concatenate([kl, kc], axis=1), jnp.concatenate([vl, vc], axis=1), scale)
    y_lat = jnp.concatenate([a_lat, b_lat], axis=-1) @ w_out
    if not ctx_out:
        return y_lat, None
    a_ctx = gdn_output(o_ctx_a, pc[1], gdn_g)
    b_ctx = blocked_attention(mla_q(pc[4], q_norm, w_uq, None), kc, vc, scale)
    y_ctx = jnp.concatenate([a_ctx, b_ctx], axis=-1) @ w_out
    return y_lat, y_ctx


def na_mixer(h_lat, h_ctx, w_qkv, rpb, w_out, ctx_out):
    b, n, _ = h_lat.shape
    rows = n // GRID_W
    kh = min(NA_KH, rows)
    heads = lambda t: t.reshape(t.shape[0], t.shape[1], NA_HEADS, NA_DH)
    ql, kl, vl = [heads(t) for t in jnp.split(h_lat @ w_qkv, 3, axis=-1)]
    qc, kc, vc = [heads(t) for t in jnp.split(h_ctx @ w_qkv, 3, axis=-1)]
    scale = NA_DH ** -0.5
    qg = ql.reshape(b, rows, GRID_W, NA_HEADS, NA_DH)
    kg = kl.reshape(b, rows, GRID_W, NA_HEADS, NA_DH)
    vg = vl.reshape(b, rows, GRID_W, NA_HEADS, NA_DH)
    cols = jnp.arange(GRID_W)
    c_start = jnp.clip(cols - NA_KW // 2, 0, GRID_W - NA_KW)
    col_mask = (cols[None, :] >= c_start[:, None]) & (cols[None, :] < c_start[:, None] + NA_KW)
    col_idx = jnp.clip(cols[None, :] - cols[:, None], -(NA_KW - 1), NA_KW - 1) + NA_KW - 1
    rpb = rpb.astype(jnp.float32)

    def one_row(r):
        r_start = jnp.clip(r - kh // 2, 0, rows - kh)
        k_r = lax.dynamic_slice_in_dim(kg, r_start, kh, axis=1)
        v_r = lax.dynamic_slice_in_dim(vg, r_start, kh, axis=1)
        q_r = lax.dynamic_index_in_dim(qg, r, axis=1, keepdims=False)
        row_idx = r_start + jnp.arange(kh) - r + NA_KH - 1
        bias = rpb[:, row_idx[None, :, None], col_idx[:, None, :]]
        s_lat = jnp.einsum('bqhd,brkhd->bhqrk', q_r, k_r).astype(jnp.float32) * scale + bias
        s_lat = jnp.where(col_mask[:, None, :], s_lat, -jnp.inf)
        s_ctx = jnp.einsum('bqhd,bkhd->bhqk', q_r, kc).astype(jnp.float32) * scale
        s = jnp.concatenate([s_lat.reshape(b, NA_HEADS, GRID_W, kh * GRID_W), s_ctx], axis=-1)
        p = jax.nn.softmax(s, axis=-1).astype(vc.dtype)
        p_lat = p[..., :kh * GRID_W].reshape(b, NA_HEADS, GRID_W, kh, GRID_W)
        return (jnp.einsum('bhqrk,brkhd->bqhd', p_lat, v_r)
                + jnp.einsum('bhqk,bkhd->bqhd', p[..., kh * GRID_W:], vc))

    o = lax.map(one_row, jnp.arange(rows))
    y_lat = jnp.moveaxis(o, 0, 1).reshape(b, n, NA_W) @ w_out
    if not ctx_out:
        return y_lat, None
    y_ctx = blocked_attention(qc, kc, vc, scale) @ w_out
    return y_lat, y_ctx


def conv_ffn(h, w_in, conv_w, conv_b, w_out):
    u, gate = jnp.split(h @ w_in, 2, axis=-1)
    gate = dwconv_seq(gate, conv_w) + conv_b
    return (jax.nn.silu(gate) * u) @ w_out


def setup_inputs(seed: int = 0) -> dict:
    key = jax.random.key(seed)
    ks = iter(jax.random.split(key, 40))
    nrm = lambda shape, s: jax.random.normal(next(ks), shape, jnp.float32) * s
    gain = lambda shape: 1.0 + nrm(shape, 0.01)
    h = GDN_HEADS
    dt = jnp.exp(jax.random.uniform(next(ks), (N_EVEN, 2, h), jnp.float32, math.log(1e-3), math.log(1e-1)))
    a_log = jnp.log(jax.random.uniform(next(ks), (N_EVEN, 2, h), jnp.float32, 1.0, 16.0))
    return {
        'x': nrm((BATCH, SEQ, D_MODEL), 1.0),
        'c': nrm((BATCH, D_MODEL), 1.0),
        'ctx': nrm((BATCH, CTX_LEN, D_MODEL), 1.0),
        'c_ctx': nrm((D_MODEL,), 1.0),
        'ada_w': nrm((DEPTH, D_MODEL, 6 * D_MODEL), 0.5 * D_MODEL ** -0.5),
        'ada_b': nrm((DEPTH, 6 * D_MODEL), 0.01),
        'norm_mix_g': gain((DEPTH, D_MODEL)),
        'norm_ffn_g': gain((DEPTH, D_MODEL)),
        'ev_w_in': nrm((N_EVEN, D_MODEL, EVEN_IN), D_MODEL ** -0.5),
        'ev_conv_qkv': nrm((N_EVEN, SHORT_CONV, GDN_QKV), SHORT_CONV ** -0.5),
        'ev_a_log': a_log,
        'ev_dt_bias': dt + jnp.log(-jnp.expm1(-dt)),
        'ev_gdn_norm': gain((N_EVEN, GDN_DV)),
        'ev_q_norm': gain((N_EVEN, MLA_Q_RANK)),
        'ev_kv_norm': gain((N_EVEN, MLA_KV_RANK)),
        'ev_w_uq': nrm((N_EVEN, MLA_Q_RANK, MLA_HEADS * (MLA_NOPE + MLA_ROPE)), MLA_Q_RANK ** -0.5),
        'ev_w_ukv': nrm((N_EVEN, MLA_KV_RANK, MLA_HEADS * (MLA_NOPE + MLA_V)), MLA_KV_RANK ** -0.5),
        'ev_w_out': nrm((N_EVEN, EVEN_MIX, D_MODEL), EVEN_MIX ** -0.5),
        'od_w_qkv': nrm((N_ODD, D_MODEL, 3 * NA_W), D_MODEL ** -0.5),
        'od_rpb': nrm((N_ODD, NA_HEADS, 2 * NA_KH - 1, 2 * NA_KW - 1), 0.02),
        'od_w_out': nrm((N_ODD, NA_W, D_MODEL), NA_W ** -0.5),
        'ffn_w_in': nrm((DEPTH, D_MODEL, 2 * D_FF), D_MODEL ** -0.5),
        'ffn_conv': nrm((DEPTH, FFN_CONV, D_FF), FFN_CONV ** -0.5),
        'ffn_conv_b': nrm((DEPTH, D_FF), 0.01),
        'ffn_w_out': nrm((DEPTH, D_FF, D_MODEL), D_FF ** -0.5),
        'final_g': gain((D_MODEL,)),
    }


def reference(x, c, ctx, c_ctx, ada_w, ada_b, norm_mix_g, norm_ffn_g, ev_w_in, ev_conv_qkv, ev_a_log,
              ev_dt_bias, ev_gdn_norm, ev_q_norm, ev_kv_norm, ev_w_uq, ev_w_ukv, ev_w_out, od_w_qkv, od_rpb,
              od_w_out, ffn_w_in, ffn_conv, ffn_conv_b, ffn_w_out, final_g):
    rope = axial_rope_tables(x.shape[1], MLA_ROPE)
    for i in range(DEPTH):
        ctx_out = i < DEPTH - 1
        sh1, sc1, g1, sh2, sc2, g2 = ada_chunks(c, ada_w[i], ada_b[i])
        csh1, csc1, cg1, csh2, csc2, cg2 = ada_chunks(c_ctx, ada_w[i], ada_b[i])
        h_lat = modulate(rmsnorm(x, norm_mix_g[i]), sh1, sc1)
        h_ctx = modulate(rmsnorm(ctx, norm_mix_g[i]), csh1, csc1)
        j = i // 2
        if i % 2 == 0:
            y_lat, y_ctx = even_mixer(h_lat, h_ctx, ev_w_in[j], ev_conv_qkv[j], ev_a_log[j], ev_dt_bias[j],
                                      ev_gdn_norm[j], ev_q_norm[j], ev_kv_norm[j], ev_w_uq[j], ev_w_ukv[j],
                                      ev_w_out[j], rope, ctx_out)
        else:
            y_lat, y_ctx = na_mixer(h_lat, h_ctx, od_w_qkv[j], od_rpb[j], od_w_out[j], ctx_out)
        x = x + g1 * y_lat
        x = x + g2 * conv_ffn(modulate(rmsnorm(x, norm_ffn_g[i]), sh2, sc2),
                              ffn_w_in[i], ffn_conv[i], ffn_conv_b[i], ffn_w_out[i])
        if ctx_out:
            ctx = ctx + cg1 * y_ctx
            ctx = ctx + cg2 * conv_ffn(modulate(rmsnorm(ctx, norm_ffn_g[i]), csh2, csc2),
                                       ffn_w_in[i], ffn_conv[i], ffn_conv_b[i], ffn_w_out[i])
    return rmsnorm(x, final_g)
```

```python
import functools
import math

import jax
import jax.numpy as jnp
from jax import lax
from jax.experimental import pallas as pl
from jax.experimental.pallas import tpu as pltpu

EPS = 1e-6
ROPE_BASE = 10000.0
GRID_W = 64

GDN_HEADS = 4
GDN_DK = 128
GDN_DV = 128
GDN_CHUNK = 64
SHORT_CONV = 5

MLA_HEADS = 4
MLA_NOPE = 128
MLA_ROPE = 64
MLA_V = 128
MLA_Q_RANK = 256
MLA_KV_RANK = 256
MLA_HEAD_PAD = 256

NA_HEADS = 16
NA_DH = 64
NA_KH = 8
NA_KW = 16
NA_PATTERNS = 8

FFN_CONV = 3
HALO = 8

VMEM_LIMIT = 56 * 1024 * 1024
NEG_BIG = -1e30

BF16 = jnp.bfloat16
F32 = jnp.float32


def _dot(a, b):
    return jnp.dot(a, b, preferred_element_type=F32)


def _dot_nt(a, b):
    return lax.dot_general(a, b, (((1,), (1,)), ((), ())), preferred_element_type=F32)


def _dot_f32(a, b):
    return jnp.dot(a, b, preferred_element_type=F32, precision=lax.Precision.HIGHEST)


def _silu(x):
    return x * jax.nn.sigmoid(x)


def _rms(x, g):
    return x * lax.rsqrt(jnp.mean(x * x, axis=-1, keepdims=True) + EPS) * g


def _params(sem):
    return pltpu.CompilerParams(dimension_semantics=sem, vmem_limit_bytes=VMEM_LIMIT)


def _const_spec(shape):
    nd = len(shape)
    return pl.BlockSpec(shape, lambda *_: (0,) * nd)


def _ada_kernel(c_ref, w_ref, b_ref, o_ref):
    s = _silu(c_ref[...]).astype(BF16)
    o_ref[0] = _dot(s, w_ref[0]) + b_ref[0]


def _ada_table(cond, ada_w, ada_b):
    depth, d, n = ada_w.shape
    r = cond.shape[0]
    tn = 1536
    return pl.pallas_call(
        _ada_kernel,
        grid=(depth, n // tn),
        in_specs=[pl.BlockSpec((r, d), lambda i, j: (0, 0)),
                  pl.BlockSpec((1, d, tn), lambda i, j: (i, 0, j)),
                  pl.BlockSpec((1, 1, tn), lambda i, j: (i, 0, j))],
        out_specs=pl.BlockSpec((1, r, tn), lambda i, j: (i, 0, j)),
        out_shape=jax.ShapeDtypeStruct((depth, r, n), F32),
        compiler_params=_params(("parallel", "parallel")),
        name="ada_table",
    )(cond, ada_w.astype(BF16), ada_b.reshape(depth, 1, n))


def _nmm_kernel(x_ref, mod_ref, g_ref, *rest, n_out, shift_idx):
    w_refs, o_refs = rest[:n_out], rest[n_out:]
    shift = mod_ref[0, shift_idx:shift_idx + 1, :]
    scale = mod_ref[0, shift_idx + 1:shift_idx + 2, :]
    h = _rms(x_ref[0], g_ref[...]) * (1.0 + scale) + shift
    hb = h.astype(BF16)
    for w_ref, o_ref in zip(w_refs, o_refs):
        o_ref[0] = _dot(hb, w_ref[...]).astype(o_ref.dtype)


def _norm_mod_matmul(xa, mods, g, ws, out_dtypes, shift_idx, tm):
    b, s, d = xa.shape
    n_out = len(ws)
    in_specs = [pl.BlockSpec((1, tm, d), lambda i, t: (i, t, 0)),
                pl.BlockSpec((1, 6, d), lambda i, t: (jnp.where(t == 0, b, i), 0, 0)),
                _const_spec((1, d))]
    in_specs += [_const_spec(w.shape) for w in ws]
    out_specs = [pl.BlockSpec((1, tm, w.shape[1]), lambda i, t: (i, t, 0)) for w in ws]
    out_shape = [jax.ShapeDtypeStruct((b, s, w.shape[1]), dt) for w, dt in zip(ws, out_dtypes)]
    return pl.pallas_call(
        functools.partial(_nmm_kernel, n_out=n_out, shift_idx=shift_idx),
        grid=(b, s // tm),
        in_specs=in_specs, out_specs=out_specs, out_shape=out_shape,
        compiler_params=_params(("parallel", "parallel")),
        name="norm_mod_matmul",
    )(xa, mods, g.reshape(1, d), *ws)


def _outproj_kernel(x_ref, mod_ref, *rest, n_in, gate_idx):
    y_refs, w_refs, o_ref = rest[:n_in], rest[n_in:2 * n_in], rest[2 * n_in]
    acc = _dot(y_refs[0][0], w_refs[0][...])
    for y_ref, w_ref in zip(y_refs[1:], w_refs[1:]):
        acc = acc + _dot(y_ref[0], w_ref[...])
    gate = mod_ref[0, gate_idx:gate_idx + 1, :]
    o_ref[0] = x_ref[0] + gate * acc


def _outproj_residual(xa, mods, ys, ws, gate_idx, tm):
    b, s, d = xa.shape
    n_in = len(ys)
    in_specs = [pl.BlockSpec((1, tm, d), lambda i, t: (i, t, 0)),
                pl.BlockSpec((1, 6, d), lambda i, t: (jnp.where(t == 0, b, i), 0, 0))]
    in_specs += [pl.BlockSpec((1, tm, y.shape[2]), lambda i, t: (i, t, 0)) for y in ys]
    in_specs += [_const_spec(w.shape) for w in ws]
    return pl.pallas_call(
        functools.partial(_outproj_kernel, n_in=n_in, gate_idx=gate_idx),
        grid=(b, s // tm),
        in_specs=in_specs,
        out_specs=pl.BlockSpec((1, tm, d), lambda i, t: (i, t, 0)),
        out_shape=jax.ShapeDtypeStruct((b, s, d), F32),
        compiler_params=_params(("parallel", "parallel")),
        name="outproj_residual",
    )(xa, mods, *ys, *ws)


def _ffn_kernel(x_ref, xp_ref, xn_ref, mod_ref, g_ref, wu_ref, wg_ref, cw_ref, cb_ref, wo_ref, o_ref,
                *, tm, n_tiles):
    t = pl.program_id(1)
    shift = mod_ref[0, 3:4, :]
    scale = mod_ref[0, 4:5, :]
    gate = mod_ref[0, 5:6, :]
    x = x_ref[0]
    x_ext = jnp.concatenate([xp_ref[0], x, xn_ref[0]], axis=0)
    h_ext = (_rms(x_ext, g_ref[...]) * (1.0 + scale) + shift).astype(BF16)
    gt = _dot(h_ext, wg_ref[...])
    u = _dot(h_ext[HALO:HALO + tm], wu_ref[...])
    prev_ok = (t >= 2).astype(F32)
    next_ok = jnp.logical_and(t >= 1, t < n_tiles - 1).astype(F32)
    row = lax.broadcasted_iota(jnp.int32, (tm + 2 * HALO, 1), 0)
    keep = jnp.where(row < HALO, prev_ok, jnp.where(row >= tm + HALO, next_ok, 1.0))
    gt = gt * keep
    g_prev = pltpu.roll(gt, 1, axis=0)[HALO:HALO + tm]
    g_next = pltpu.roll(gt, tm + 2 * HALO - 1, axis=0)[HALO:HALO + tm]
    g_mid = gt[HALO:HALO + tm]
    conv = g_prev * cw_ref[0:1, :] + g_mid * cw_ref[1:2, :] + g_next * cw_ref[2:3, :] + cb_ref[...]
    act = (_silu(conv) * u).astype(BF16)
    o_ref[0] = x + gate * _dot(act, wo_ref[...])


def _conv_ffn_residual(xa, mods, g, wu, wg, conv_w, conv_b, wo, tm):
    b, s, d = xa.shape
    f = wu.shape[1]
    n_tiles = s // tm
    hb = tm // HALO
    last = s // HALO - 1
    return pl.pallas_call(
        functools.partial(_ffn_kernel, tm=tm, n_tiles=n_tiles),
        grid=(b, n_tiles),
        in_specs=[pl.BlockSpec((1, tm, d), lambda i, t: (i, t, 0)),
                  pl.BlockSpec((1, HALO, d), lambda i, t: (i, jnp.maximum(t * hb - 1, 0), 0)),
                  pl.BlockSpec((1, HALO, d), lambda i, t: (i, jnp.minimum((t + 1) * hb, last), 0)),
                  pl.BlockSpec((1, 6, d), lambda i, t: (jnp.where(t == 0, b, i), 0, 0)),
                  _const_spec((1, d)),
                  _const_spec((d, f)), _const_spec((d, f)),
                  _const_spec((FFN_CONV, f)), _const_spec((1, f)),
                  _const_spec((f, d))],
        out_specs=pl.BlockSpec((1, tm, d), lambda i, t: (i, t, 0)),
        out_shape=jax.ShapeDtypeStruct((b, s, d), F32),
        compiler_params=_params(("parallel", "parallel")),
        name="conv_ffn",
    )(xa, xa, xa, mods, g.reshape(1, d), wu, wg, conv_w, conv_b.reshape(1, f), wo)


def _mla_up_kernel(p_ref, cq_ref, sq_ref, ck_ref, sk_ref, qn_ref, kvn_ref,
                   wqa_ref, wqb_ref, wk_ref, wv_ref, pk_ref, q_ref, k_ref, v_ref, *, scale):
    r = MLA_Q_RANK
    cq = _rms(p_ref[0, :, 0:r], qn_ref[...]).astype(BF16)
    ckv = _rms(p_ref[0, :, r:2 * r], kvn_ref[...]).astype(BF16)
    kr = p_ref[0, :, 2 * r:2 * r + 128]
    krr = p_ref[0, :, 2 * r + 128:2 * r + 256]
    k_rope = (kr * ck_ref[...] + krr * sk_ref[...]).astype(BF16)
    k_ref[0] = (_dot(ckv, wk_ref[...]) + _dot(k_rope, pk_ref[...])).astype(BF16)
    v_ref[0] = _dot(ckv, wv_ref[...]).astype(BF16)
    qa = _dot(cq, wqa_ref[...])
    qb = _dot(cq, wqb_ref[...])
    cq_t = cq_ref[...] * scale
    sq_t = sq_ref[...] * scale
    for h in range(MLA_HEADS):
        sl = slice(h * MLA_HEAD_PAD, (h + 1) * MLA_HEAD_PAD)
        q_ref[0, :, sl] = (qa[:, sl] * cq_t + qb[:, sl] * sq_t).astype(BF16)


def _mla_up(p, tabs, q_norm, kv_norm, wqa, wqb, wk, wv, pk, tm):
    b, s, pw = p.shape
    cosq, sinq, cosk, sink = tabs
    hw = MLA_HEADS * MLA_HEAD_PAD
    vw = MLA_HEADS * MLA_V
    scale = (MLA_NOPE + MLA_ROPE) ** -0.5
    row = lambda w: pl.BlockSpec((tm, w), lambda i, t: (t, 0))
    return pl.pallas_call(
        functools.partial(_mla_up_kernel, scale=scale),
        grid=(b, s // tm),
        in_specs=[pl.BlockSpec((1, tm, pw), lambda i, t: (i, t, 0)),
                  row(MLA_HEAD_PAD), row(MLA_HEAD_PAD), row(128), row(128),
                  _const_spec((1, MLA_Q_RANK)), _const_spec((1, MLA_KV_RANK)),
                  _const_spec(wqa.shape), _const_spec(wqb.shape), _const_spec(wk.shape),
                  _const_spec(wv.shape), _const_spec(pk.shape)],
        out_specs=[pl.BlockSpec((1, tm, hw), lambda i, t: (i, t, 0)),
                   pl.BlockSpec((1, tm, hw), lambda i, t: (i, t, 0)),
                   pl.BlockSpec((1, tm, vw), lambda i, t: (i, t, 0))],
        out_shape=[jax.ShapeDtypeStruct((b, s, hw), BF16),
                   jax.ShapeDtypeStruct((b, s, hw), BF16),
                   jax.ShapeDtypeStruct((b, s, vw), BF16)],
        compiler_params=_params(("parallel", "parallel")),
        name="mla_up",
    )(p, cosq, sinq, cosk, sink, q_norm.reshape(1, -1), kv_norm.reshape(1, -1), wqa, wqb, wk, wv, pk)


def _softmax_pv(s, v):
    m = jnp.max(s, axis=-1, keepdims=True)
    p = jnp.exp(s - m)
    l = jnp.sum(p, axis=-1, keepdims=True)
    return _dot(p.astype(BF16), v) / l


def _mla_attn_kernel(q_ref, k_ref, v_ref, o_ref, *, ctx):
    t = pl.program_id(2)
    q = q_ref[0]

    @pl.when(t == 0)
    def _():
        o_ref[0] = _softmax_pv(_dot_nt(q, k_ref[0, 0:ctx, :]), v_ref[0, 0:ctx, :]).astype(o_ref.dtype)

    @pl.when(t > 0)
    def _():
        o_ref[0] = _softmax_pv(_dot_nt(q, k_ref[0]), v_ref[0]).astype(o_ref.dtype)


def _mla_attention(q, k, v, ctx, tq):
    b, s, _ = q.shape
    return pl.pallas_call(
        functools.partial(_mla_attn_kernel, ctx=ctx),
        grid=(b, MLA_HEADS, s // tq),
        in_specs=[pl.BlockSpec((1, tq, MLA_HEAD_PAD), lambda i, h, t: (i, t, h)),
                  pl.BlockSpec((1, s, MLA_HEAD_PAD), lambda i, h, t: (i, 0, h)),
                  pl.BlockSpec((1, s, MLA_V), lambda i, h, t: (i, 0, h))],
        out_specs=pl.BlockSpec((1, tq, MLA_V), lambda i, h, t: (i, t, h)),
        out_shape=jax.ShapeDtypeStruct((b, s, MLA_HEADS * MLA_V), BF16),
        compiler_params=_params(("parallel", "parallel", "parallel")),
        name="mla_attention",
    )(q, k, v)


def _na_kernel(q_ref, k_ref, v_ref, bias_ref, o_ref, *, ctx, rows):
    w = GRID_W
    win = NA_KH * w
    lane = lax.broadcasted_iota(jnp.int32, (w, 2 * NA_DH), 1)
    first = lane < NA_DH
    kc = k_ref[0, 0:ctx, :]
    vc = v_ref[0, 0:ctx, :]

    def split_heads(qr):
        zero = jnp.zeros_like(qr)
        return jnp.concatenate([jnp.where(first, qr, zero), jnp.where(first, zero, qr)], axis=0)

    def merge_heads(o2):
        n = o2.shape[0] // 2
        lane_n = lax.broadcasted_iota(jnp.int32, (n, 2 * NA_DH), 1)
        return jnp.where(lane_n < NA_DH, o2[:n], o2[n:])

    qc = q_ref[0, 0:ctx, :]
    zc = jnp.zeros_like(qc)
    lane_c = lax.broadcasted_iota(jnp.int32, (ctx, 2 * NA_DH), 1)
    qc2 = jnp.concatenate([jnp.where(lane_c < NA_DH, qc, zc), jnp.where(lane_c < NA_DH, zc, qc)], axis=0)
    o_ref[0, 0:ctx, :] = merge_heads(_softmax_pv(_dot_nt(qc2, kc), vc)).astype(o_ref.dtype)

    def one_row(r, carry):
        r_start = jnp.clip(r - NA_KH // 2, 0, rows - NA_KH)
        pat = jnp.where(r < NA_KH // 2, r,
                        jnp.where(r <= rows - NA_KH // 2, NA_KH // 2, r - (rows - NA_KH)))
        q0 = pl.multiple_of(ctx + r * w, w)
        k0 = pl.multiple_of(ctx + r_start * w, w)
        q2 = split_heads(q_ref[0, pl.ds(q0, w), :])
        kw = k_ref[0, pl.ds(k0, win), :]
        vw = v_ref[0, pl.ds(k0, win), :]
        s_lat = _dot_nt(q2, kw) + bias_ref[0, pat]
        s_ctx = _dot_nt(q2, kc)
        m = jnp.maximum(jnp.max(s_lat, axis=-1, keepdims=True), jnp.max(s_ctx, axis=-1, keepdims=True))
        p_lat = jnp.exp(s_lat - m)
        p_ctx = jnp.exp(s_ctx - m)
        l = jnp.sum(p_lat, axis=-1, keepdims=True) + jnp.sum(p_ctx, axis=-1, keepdims=True)
        o2 = (_dot(p_lat.astype(BF16), vw) + _dot(p_ctx.astype(BF16), vc)) / l
        o_ref[0, pl.ds(q0, w), :] = merge_heads(o2).astype(o_ref.dtype)
        return carry

    lax.fori_loop(0, rows, one_row, 0)


def _na_attention(q, k, v, bias, ctx):
    b, s, width = q.shape
    rows = (s - ctx) // GRID_W
    pairs = width // (2 * NA_DH)
    blk = pl.BlockSpec((1, s, 2 * NA_DH), lambda i, h: (i, 0, h))
    return pl.pallas_call(
        functools.partial(_na_kernel, ctx=ctx, rows=rows),
        grid=(b, pairs),
        in_specs=[blk, blk, blk,
                  pl.BlockSpec((1, NA_PATTERNS, 2 * GRID_W, NA_KH * GRID_W), lambda i, h: (h, 0, 0, 0))],
        out_specs=blk,
        out_shape=jax.ShapeDtypeStruct((b, s, width), BF16),
        compiler_params=_params(("parallel", "parallel")),
        name="na_attention",
    )(q, k, v, bias)


def _na_bias_table(rpb, rows):
    w, kh, kw = GRID_W, NA_KH, NA_KW
    cols = jnp.arange(w)
    c_start = jnp.clip(cols - kw // 2, 0, w - kw)
    col_mask = (cols[None, :] >= c_start[:, None]) & (cols[None, :] < c_start[:, None] + kw)
    col_idx = jnp.clip(cols[None, :] - cols[:, None], -(kw - 1), kw - 1) + kw - 1
    rep = jnp.array([0, 1, 2, 3, 4, rows - 3, rows - 2, rows - 1])
    r_start = jnp.clip(rep - kh // 2, 0, rows - kh)
    row_idx = r_start[:, None] + jnp.arange(kh)[None, :] - rep[:, None] + kh - 1
    t = rpb.astype(F32)[:, row_idx[:, None, :, None], col_idx[None, :, None, :]]
    t = jnp.where(col_mask[None, None, :, None, :], t, NEG_BIG)
    h = rpb.shape[0]
    t = t.reshape(h // 2, 2, NA_PATTERNS, w, kh * w).transpose(0, 2, 1, 3, 4)
    return t.reshape(h // 2, NA_PATTERNS, 2 * w, kh * w)


def _gdn_kernel(q_ref, k_ref, v_ref, z_ref, ba_ref, cwq_ref, cwk_ref, cwv_ref, alog_ref, dtb_ref, gain_ref,
                o_ref, qs, ks, vs, bs, gs, w_s, ut_s, qk_s, qd_s, kdt_s, gl_s, o_s, *, ctx, seq):
    c = GDN_CHUNK
    dk = GDN_DK
    nc = seq // c
    ncc = ctx // c
    head = pl.program_id(1)

    row = lax.broadcasted_iota(jnp.int32, (seq, 128), 0)
    pad = SHORT_CONV // 2

    def short_conv(x_ref, cw_ref):
        x = x_ref[0]
        acc = x * cw_ref[pad:pad + 1, :]
        for j in range(SHORT_CONV):
            d = j - pad
            if d == 0:
                continue
            src = row + d
            ok = (src >= 0) & (src < seq) & ((row < ctx) == (src < ctx))
            xs = pltpu.roll(x, (-d) % seq, axis=0)
            acc = acc + jnp.where(ok, xs, 0.0) * cw_ref[j:j + 1, :]
        return _silu(acc)

    def l2n(x):
        return x * lax.rsqrt(jnp.sum(x * x, axis=-1, keepdims=True) + EPS)

    qs[...] = l2n(short_conv(q_ref, cwq_ref)) * (dk ** -0.5)
    ks[...] = l2n(short_conv(k_ref, cwk_ref))
    vs[...] = short_conv(v_ref, cwv_ref)

    ba = ba_ref[0]
    lane = lax.broadcasted_iota(jnp.int32, (seq, 128), 1)
    beta_all = jax.nn.sigmoid(ba)
    sp = jnp.maximum(ba + dtb_ref[...], 0.0) + jnp.log1p(jnp.exp(-jnp.abs(ba + dtb_ref[...])))
    g_all = -jnp.exp(alog_ref[...]) * sp
    for d in range(2):
        bcol = jnp.sum(jnp.where(lane == d * GDN_HEADS + head, beta_all, 0.0), axis=-1, keepdims=True)
        gcol = jnp.sum(jnp.where(lane == 2 * GDN_HEADS + d * GDN_HEADS + head, g_all, 0.0),
                       axis=-1, keepdims=True)
        bs[d] = jnp.broadcast_to(bcol, (seq, 128))
        gs[d] = jnp.broadcast_to(gcol, (seq, 128))

    ii = lax.broadcasted_iota(jnp.int32, (c, c), 0)
    jj = lax.broadcasted_iota(jnp.int32, (c, c), 1)
    eye = (ii == jj).astype(F32)
    ones_cc = jnp.ones((c, c), F32)

    def inverse(a_mat):
        same8 = (ii // 8) == (jj // 8)
        p = jnp.where(same8, -a_mat, 0.0)
        t = eye + p
        p2 = _dot_f32(p, p)
        t = t + _dot_f32(t, p2)
        p4 = _dot_f32(p2, p2)
        t = t + _dot_f32(t, p4)
        for blk in (8, 16, 32):
            off = ((ii // (2 * blk)) == (jj // (2 * blk))) & ((ii // blk) != (jj // blk))
            l_mat = jnp.where(off, a_mat, 0.0)
            t = t - _dot_f32(t, _dot_f32(l_mat, t))
        return t

    for d in range(2):
        incl = (ii >= jj) if d == 0 else (ii <= jj)
        strict = (ii > jj) if d == 0 else (ii < jj)
        tri = incl.astype(F32)

        def prep(n, carry, incl=incl, strict=strict, tri=tri, d=d):
            r0 = pl.multiple_of(n * c, c)
            q = qs[pl.ds(r0, c), :]
            k = ks[pl.ds(r0, c), :]
            v = vs[pl.ds(r0, c), :]
            bb = bs[d, pl.ds(r0, c), :]
            gb = gs[d, pl.ds(r0, c), :]
            gc = _dot_f32(tri, gb)
            glast = _dot_f32(ones_cc, gb)
            g_i = gc[:, 0:c]
            g_j = gc.T[0:c, :]
            decay = jnp.where(incl, jnp.exp(jnp.where(incl, g_i - g_j, 0.0)), 0.0)
            kb = k.astype(BF16)
            kk = _dot_nt(kb, kb)
            a_mat = jnp.where(strict, bb[:, 0:c] * kk * decay, 0.0)
            t = inverse(a_mat).astype(BF16)
            eg = jnp.exp(gc)
            w_s[n] = _dot(t, (k * (bb * eg)).astype(BF16)).astype(BF16)
            ut_s[n] = _dot(t, (v * bb).astype(BF16))
            qk_s[n] = (_dot_nt(q.astype(BF16), kb) * decay).astype(BF16)
            qd_s[n] = (q * eg).astype(BF16)
            kdt_s[n] = (k * jnp.exp(glast - gc)).T.astype(BF16)
            gl_s[n] = jnp.exp(glast[0:8, :])
            return carry

        lax.fori_loop(0, nc, prep, 0)

        def step(i, state, d=d):
            if d == 0:
                n = i
            else:
                n = jnp.where(i < ncc, ncc - 1 - i, nc - 1 - (i - ncc))
            sb = state.astype(BF16)
            u = ut_s[n] - _dot(w_s[n], sb)
            ub = u.astype(BF16)
            o = _dot(qd_s[n], sb) + _dot(qk_s[n], ub)
            r0 = pl.multiple_of(n * c, c)
            if d == 0:
                o_s[pl.ds(r0, c), :] = o
            else:
                o_s[pl.ds(r0, c), :] += o
            return gl_s[n][0:1, :] * state + _dot(kdt_s[n], ub)

        lax.fori_loop(0, nc, step, jnp.zeros((dk, GDN_DV), F32))

    o = _rms(o_s[...], gain_ref[...]) * _silu(z_ref[0])
    o_ref[0] = o.astype(o_ref.dtype)


def _gdn(qkv, z, ba, conv_w, alog_row, dtb_row, gain, ctx):
    b, s, _ = qkv.shape
    h = GDN_HEADS
    nc = s // GDN_CHUNK
    c = GDN_CHUNK
    col = lambda off: pl.BlockSpec((1, s, 128), lambda i, j: (i, 0, off + j))
    cw = lambda off: pl.BlockSpec((SHORT_CONV, 128), lambda i, j: (0, off + j))
    scratch = [pltpu.VMEM((s, 128), F32), pltpu.VMEM((s, 128), F32), pltpu.VMEM((s, 128), F32),
               pltpu.VMEM((2, s, 128), F32), pltpu.VMEM((2, s, 128), F32),
               pltpu.VMEM((nc, c, 128), BF16), pltpu.VMEM((nc, c, 128), F32),
               pltpu.VMEM((nc, c, c), BF16), pltpu.VMEM((nc, c, 128), BF16),
               pltpu.VMEM((nc, 128, c), BF16), pltpu.VMEM((nc, 8, 128), F32),
               pltpu.VMEM((s, 128), F32)]
    return pl.pallas_call(
        functools.partial(_gdn_kernel, ctx=ctx, seq=s),
        grid=(b, h),
        in_specs=[col(0), col(h), col(2 * h),
                  pl.BlockSpec((1, s, 128), lambda i, j: (i, 0, j)),
                  pl.BlockSpec((1, s, 128), lambda i, j: (i, 0, 0)),
                  cw(0), cw(h), cw(2 * h),
                  _const_spec((1, 128)), _const_spec((1, 128)), _const_spec((1, 128))],
        out_specs=pl.BlockSpec((1, s, 128), lambda i, j: (i, 0, j)),
        out_shape=jax.ShapeDtypeStruct((b, s, h * GDN_DV), BF16),
        scratch_shapes=scratch,
        compiler_params=_params(("parallel", "parallel")),
        name="gdn",
    )(qkv, qkv, qkv, z, ba, conv_w, conv_w, conv_w, alog_row, dtb_row, gain.reshape(1, -1))


def _final_kernel(x_ref, g_ref, o_ref):
    o_ref[0] = _rms(x_ref[0], g_ref[...])


def _final_norm(xa, g, ctx, tm):
    b, s, d = xa.shape
    off = ctx // tm
    return pl.pallas_call(
        _final_kernel,
        grid=(b, (s - ctx) // tm),
        in_specs=[pl.BlockSpec((1, tm, d), lambda i, t: (i, t + off, 0)), _const_spec((1, d))],
        out_specs=pl.BlockSpec((1, tm, d), lambda i, t: (i, t, 0)),
        out_shape=jax.ShapeDtypeStruct((b, s - ctx, d), F32),
        compiler_params=_params(("parallel", "parallel")),
        name="final_norm",
    )(xa, g.reshape(1, d))


def _rope_tables(n_lat, ctx):
    t = jnp.arange(n_lat)
    row = (t // GRID_W).astype(F32)
    col = (t % GRID_W).astype(F32)
    quarter = MLA_ROPE // 4
    inv_freq = ROPE_BASE ** (-jnp.arange(quarter, dtype=F32) / quarter)
    ang_r = row[:, None] * inv_freq
    ang_c = col[:, None] * inv_freq
    ang = jnp.concatenate([ang_r, ang_r, ang_c, ang_c], axis=-1)
    cos = jnp.concatenate([jnp.ones((ctx, MLA_ROPE), F32), jnp.cos(ang)], axis=0)
    sin = jnp.concatenate([jnp.zeros((ctx, MLA_ROPE), F32), jnp.sin(ang)], axis=0)
    s = ctx + n_lat
    one = jnp.ones((s, MLA_NOPE), F32)
    zn = jnp.zeros((s, MLA_NOPE), F32)
    zp = jnp.zeros((s, MLA_HEAD_PAD - MLA_NOPE - MLA_ROPE), F32)
    cosq = jnp.concatenate([one, cos, zp], axis=1)
    sinq = jnp.concatenate([zn, sin, zp], axis=1)
    zk = jnp.zeros((s, 128 - MLA_ROPE), F32)
    cosk = jnp.concatenate([cos, zk], axis=1)
    sink = jnp.concatenate([sin, zk], axis=1)
    return cosq, sinq, cosk, sink


def _rot_cols(w):
    q = MLA_ROPE // 4
    a, b, c, d = w[..., 0:q], w[..., q:2 * q], w[..., 2 * q:3 * q], w[..., 3 * q:4 * q]
    return jnp.concatenate([-b, a, -d, c], axis=-1)


def _even_weights(w_in, w_uq, w_ukv, w_out):
    d = w_in.shape[0]
    h = GDN_HEADS
    qkv_w = 2 * h * GDN_DK + h * GDN_DV
    vw = h * GDN_DV
    o = 0
    w_qkv = w_in[:, o:o + qkv_w]; o += qkv_w
    w_z = w_in[:, o:o + vw]; o += vw
    w_ba = w_in[:, o:o + 4 * h]; o += 4 * h
    w_cq = w_in[:, o:o + MLA_Q_RANK]; o += MLA_Q_RANK
    w_ckv = w_in[:, o:o + MLA_KV_RANK]; o += MLA_KV_RANK
    w_kr = w_in[:, o:o + MLA_ROPE]
    z64 = jnp.zeros((d, 128 - MLA_ROPE), w_in.dtype)
    w_ba = jnp.concatenate([w_ba, jnp.zeros((d, 128 - 4 * h), w_in.dtype)], axis=1)
    w_mla = jnp.concatenate([w_cq, w_ckv, w_kr, z64, _rot_cols(w_kr), z64], axis=1)
    proj = [w_qkv.astype(BF16), w_z.astype(BF16), w_ba.astype(BF16), w_mla.astype(BF16)]

    rq = MLA_Q_RANK
    wq = w_uq.reshape(rq, MLA_HEADS, MLA_NOPE + MLA_ROPE)
    zpad = jnp.zeros((rq, MLA_HEADS, MLA_HEAD_PAD - MLA_NOPE - MLA_ROPE), w_uq.dtype)
    znope = jnp.zeros((rq, MLA_HEADS, MLA_NOPE), w_uq.dtype)
    wqa = jnp.concatenate([wq, zpad], axis=-1).reshape(rq, -1)
    wqb = jnp.concatenate([znope, _rot_cols(wq[..., MLA_NOPE:]), zpad], axis=-1).reshape(rq, -1)
    rk = MLA_KV_RANK
    wkv = w_ukv.reshape(rk, MLA_HEADS, MLA_NOPE + MLA_V)
    wk = jnp.concatenate([wkv[..., :MLA_NOPE], jnp.zeros((rk, MLA_HEADS, MLA_HEAD_PAD - MLA_NOPE), w_ukv.dtype)],
                         axis=-1).reshape(rk, -1)
    wv = wkv[..., MLA_NOPE:].reshape(rk, -1)
    eye = jnp.eye(MLA_ROPE, dtype=F32)
    blk = jnp.concatenate([jnp.zeros((MLA_ROPE, MLA_NOPE), F32), eye,
                           jnp.zeros((MLA_ROPE, MLA_HEAD_PAD - MLA_NOPE - MLA_ROPE), F32)], axis=1)
    pk = jnp.concatenate([jnp.tile(blk, (1, MLA_HEADS)),
                          jnp.zeros((128 - MLA_ROPE, MLA_HEADS * MLA_HEAD_PAD), F32)], axis=0)
    up = [wqa.astype(BF16), wqb.astype(BF16), wk.astype(BF16), wv.astype(BF16), pk.astype(BF16)]
    out = [w_out[:vw].astype(BF16), w_out[vw:].astype(BF16)]
    return proj, up, out


def _gate_rows(a_log, dt_bias):
    n = 2 * GDN_HEADS
    z = jnp.zeros((n,), F32)
    tail = jnp.zeros((128 - 2 * n,), F32)
    alog_row = jnp.concatenate([z, a_log.astype(F32).reshape(n), tail]).reshape(1, 128)
    dtb_row = jnp.concatenate([z, dt_bias.astype(F32).reshape(n), tail]).reshape(1, 128)
    return alog_row, dtb_row


def kernel(x, c, ctx, c_ctx, ada_w, ada_b, norm_mix_g, norm_ffn_g, ev_w_in, ev_conv_qkv, ev_a_log, ev_dt_bias,
           ev_gdn_norm, ev_q_norm, ev_kv_norm, ev_w_uq, ev_w_ukv, ev_w_out, od_w_qkv, od_rpb, od_w_out,
           ffn_w_in, ffn_conv, ffn_conv_b, ffn_w_out, final_g):
    b, n_lat, d = x.shape
    n_ctx = ctx.shape[1]
    depth = ada_w.shape[0]
    tm = n_ctx
    assert n_lat % tm == 0 and n_lat % GRID_W == 0 and tm % GDN_CHUNK == 0
    rows = n_lat // GRID_W
    assert rows >= NA_KH

    r_pad = -(b + 1) % 8
    cond = jnp.concatenate([c, c_ctx[None, :], jnp.zeros((r_pad, d), c.dtype)], axis=0)
    mods = _ada_table(cond, ada_w, ada_b).reshape(depth, b + 1 + r_pad, 6, d)
    tabs = _rope_tables(n_lat, n_ctx)
    d_ff = ffn_w_out.shape[1]

    xa = jnp.concatenate([ctx, x], axis=1)
    for i in range(depth):
        j = i // 2
        m = mods[i]
        if i % 2 == 0:
            proj_w, up_w, out_w = _even_weights(ev_w_in[j], ev_w_uq[j], ev_w_ukv[j], ev_w_out[j])
            qkv, z, ba, p_mla = _norm_mod_matmul(xa, m, norm_mix_g[i], proj_w, [F32, F32, F32, F32], 0, tm)
            alog_row, dtb_row = _gate_rows(ev_a_log[j], ev_dt_bias[j])
            y_a = _gdn(qkv, z, ba, ev_conv_qkv[j], alog_row, dtb_row, ev_gdn_norm[j], n_ctx)
            q, k, v = _mla_up(p_mla, tabs, ev_q_norm[j], ev_kv_norm[j], *up_w, tm)
            y_b = _mla_attention(q, k, v, n_ctx, tm)
            xa = _outproj_residual(xa, m, [y_a, y_b], out_w, 2, tm)
        else:
            w = od_w_qkv[j]
            nw = w.shape[1] // 3
            ws = [(w[:, :nw] * (NA_DH ** -0.5)).astype(BF16), w[:, nw:2 * nw].astype(BF16),
                  w[:, 2 * nw:].astype(BF16)]
            q, k, v = _norm_mod_matmul(xa, m, norm_mix_g[i], ws, [BF16, BF16, BF16], 0, tm)
            y = _na_attention(q, k, v, _na_bias_table(od_rpb[j], rows), n_ctx)
            xa = _outproj_residual(xa, m, [y], [od_w_out[j].astype(BF16)], 2, tm)
        wi = ffn_w_in[i]
        xa = _conv_ffn_residual(xa, m, norm_ffn_g[i], wi[:, :d_ff].astype(BF16), wi[:, d_ff:].astype(BF16),
                                ffn_conv[i], ffn_conv_b[i], ffn_w_out[i].astype(BF16), tm)
    return _final_norm(xa, final_g, n_ctx, tm)
```

```python
import functools
import math

import jax
import jax.numpy as jnp
from jax import lax
from jax.experimental import pallas as pl
from jax.experimental.pallas import tpu as pltpu

EPS = 1e-6
ROPE_BASE = 10000.0
GRID_W = 64

GDN_HEADS = 4
GDN_DK = 128
GDN_DV = 128
GDN_CHUNK = 64
GDN_GROUP = 4
SHORT_CONV = 5

MLA_HEADS = 4
MLA_NOPE = 128
MLA_ROPE = 64
MLA_V = 128
MLA_Q_RANK = 256
MLA_KV_RANK = 256
MLA_HEAD_PAD = 256

NA_HEADS = 16
NA_DH = 64
NA_KH = 8
NA_KW = 16
NA_PATTERNS = 8

FFN_CONV = 3
HALO = 8

VMEM_LIMIT = 56 * 1024 * 1024
NEG_BIG = -1e30

BF16 = jnp.bfloat16
F32 = jnp.float32


def _dot(a, b):
    return jnp.dot(a, b, preferred_element_type=F32)


def _dot_nt(a, b):
    return lax.dot_general(a, b, (((1,), (1,)), ((), ())), preferred_element_type=F32)


def _silu(x):
    return x * jax.nn.sigmoid(x)


def _softplus(x):
    return jnp.maximum(x, 0.0) + jnp.log1p(jnp.exp(-jnp.abs(x)))


def _rms(x, g):
    return x * lax.rsqrt(jnp.mean(x * x, axis=-1, keepdims=True) + EPS) * g


def _params(sem):
    return pltpu.CompilerParams(dimension_semantics=sem, vmem_limit_bytes=VMEM_LIMIT)


def _const_spec(shape):
    nd = len(shape)
    return pl.BlockSpec(shape, lambda *_: (0,) * nd)


def _ada_kernel(c_ref, w_ref, b_ref, o_ref):
    s = _silu(c_ref[...]).astype(BF16)
    o_ref[0] = _dot(s, w_ref[0]) + b_ref[0]


def _ada_table(cond, ada_w, ada_b):
    depth, d, n = ada_w.shape
    r = cond.shape[0]
    tn = 1536
    return pl.pallas_call(
        _ada_kernel,
        grid=(depth, n // tn),
        in_specs=[pl.BlockSpec((r, d), lambda i, j: (0, 0)),
                  pl.BlockSpec((1, d, tn), lambda i, j: (i, 0, j)),
                  pl.BlockSpec((1, 1, tn), lambda i, j: (i, 0, j))],
        out_specs=pl.BlockSpec((1, r, tn), lambda i, j: (i, 0, j)),
        out_shape=jax.ShapeDtypeStruct((depth, r, n), F32),
        compiler_params=_params(("parallel", "parallel")),
        name="ada_table",
    )(cond, ada_w.astype(BF16), ada_b.reshape(depth, 1, n))


def _nmm_kernel(x_ref, mod_ref, g_ref, *rest, n_out, shift_idx):
    w_refs, o_refs = rest[:n_out], rest[n_out:]
    shift = mod_ref[0, shift_idx:shift_idx + 1, :]
    scale = mod_ref[0, shift_idx + 1:shift_idx + 2, :]
    h = _rms(x_ref[0], g_ref[...]) * (1.0 + scale) + shift
    hb = h.astype(BF16)
    for w_ref, o_ref in zip(w_refs, o_refs):
        o_ref[0] = _dot(hb, w_ref[...]).astype(o_ref.dtype)


def _norm_mod_matmul(xa, mods, g, ws, out_dtypes, shift_idx, tm):
    b, s, d = xa.shape
    n_out = len(ws)
    in_specs = [pl.BlockSpec((1, tm, d), lambda i, t: (i, t, 0)),
                pl.BlockSpec((1, 6, d), lambda i, t: (jnp.where(t == 0, b, i), 0, 0)),
                _const_spec((1, d))]
    in_specs += [_const_spec(w.shape) for w in ws]
    out_specs = [pl.BlockSpec((1, tm, w.shape[1]), lambda i, t: (i, t, 0)) for w in ws]
    out_shape = [jax.ShapeDtypeStruct((b, s, w.shape[1]), dt) for w, dt in zip(ws, out_dtypes)]
    return pl.pallas_call(
        functools.partial(_nmm_kernel, n_out=n_out, shift_idx=shift_idx),
        grid=(b, s // tm),
        in_specs=in_specs, out_specs=out_specs, out_shape=out_shape,
        compiler_params=_params(("parallel", "parallel")),
        name="norm_mod_matmul",
    )(xa, mods, g.reshape(1, d), *ws)


def _outproj_kernel(x_ref, mod_ref, *rest, n_in, gate_idx):
    y_refs, w_refs, o_ref = rest[:n_in], rest[n_in:2 * n_in], rest[2 * n_in]
    acc = _dot(y_refs[0][0], w_refs[0][...])
    for y_ref, w_ref in zip(y_refs[1:], w_refs[1:]):
        acc = acc + _dot(y_ref[0], w_ref[...])
    gate = mod_ref[0, gate_idx:gate_idx + 1, :]
    o_ref[0] = x_ref[0] + gate * acc


def _outproj_residual(xa, mods, ys, ws, gate_idx, tm):
    b, s, d = xa.shape
    n_in = len(ys)
    in_specs = [pl.BlockSpec((1, tm, d), lambda i, t: (i, t, 0)),
                pl.BlockSpec((1, 6, d), lambda i, t: (jnp.where(t == 0, b, i), 0, 0))]
    in_specs += [pl.BlockSpec((1, tm, y.shape[2]), lambda i, t: (i, t, 0)) for y in ys]
    in_specs += [_const_spec(w.shape) for w in ws]
    return pl.pallas_call(
        functools.partial(_outproj_kernel, n_in=n_in, gate_idx=gate_idx),
        grid=(b, s // tm),
        in_specs=in_specs,
        out_specs=pl.BlockSpec((1, tm, d), lambda i, t: (i, t, 0)),
        out_shape=jax.ShapeDtypeStruct((b, s, d), F32),
        compiler_params=_params(("parallel", "parallel")),
        name="outproj_residual",
    )(xa, mods, *ys, *ws)


def _ffn_kernel(x_ref, xp_ref, xn_ref, mod_ref, g_ref, wu_ref, wg_ref, cw_ref, cb_ref, wo_ref, o_ref,
                *, tm, n_tiles):
    t = pl.program_id(1)
    shift = mod_ref[0, 3:4, :]
    scale = mod_ref[0, 4:5, :]
    gate = mod_ref[0, 5:6, :]
    x = x_ref[0]
    x_ext = jnp.concatenate([xp_ref[0], x, xn_ref[0]], axis=0)
    h_ext = (_rms(x_ext, g_ref[...]) * (1.0 + scale) + shift).astype(BF16)
    gt = _dot(h_ext, wg_ref[...])
    u = _dot(h_ext[HALO:HALO + tm], wu_ref[...])
    prev_ok = (t >= 2).astype(F32)
    next_ok = jnp.logical_and(t >= 1, t < n_tiles - 1).astype(F32)
    row = lax.broadcasted_iota(jnp.int32, (tm + 2 * HALO, 1), 0)
    keep = jnp.where(row < HALO, prev_ok, jnp.where(row >= tm + HALO, next_ok, 1.0))
    gt = gt * keep
    g_prev = pltpu.roll(gt, 1, axis=0)[HALO:HALO + tm]
    g_next = pltpu.roll(gt, tm + 2 * HALO - 1, axis=0)[HALO:HALO + tm]
    g_mid = gt[HALO:HALO + tm]
    conv = g_prev * cw_ref[0:1, :] + g_mid * cw_ref[1:2, :] + g_next * cw_ref[2:3, :] + cb_ref[...]
    act = (_silu(conv) * u).astype(BF16)
    o_ref[0] = x + gate * _dot(act, wo_ref[...])


def _conv_ffn_residual(xa, mods, g, wu, wg, conv_w, conv_b, wo, tm):
    b, s, d = xa.shape
    f = wu.shape[1]
    n_tiles = s // tm
    hb = tm // HALO
    last = s // HALO - 1
    return pl.pallas_call(
        functools.partial(_ffn_kernel, tm=tm, n_tiles=n_tiles),
        grid=(b, n_tiles),
        in_specs=[pl.BlockSpec((1, tm, d), lambda i, t: (i, t, 0)),
                  pl.BlockSpec((1, HALO, d), lambda i, t: (i, jnp.maximum(t * hb - 1, 0), 0)),
                  pl.BlockSpec((1, HALO, d), lambda i, t: (i, jnp.minimum((t + 1) * hb, last), 0)),
                  pl.BlockSpec((1, 6, d), lambda i, t: (jnp.where(t == 0, b, i), 0, 0)),
                  _const_spec((1, d)),
                  _const_spec((d, f)), _const_spec((d, f)),
                  _const_spec((FFN_CONV, f)), _const_spec((1, f)),
                  _const_spec((f, d))],
        out_specs=pl.BlockSpec((1, tm, d), lambda i, t: (i, t, 0)),
        out_shape=jax.ShapeDtypeStruct((b, s, d), F32),
        compiler_params=_params(("parallel", "parallel")),
        name="conv_ffn",
    )(xa, xa, xa, mods, g.reshape(1, d), wu, wg, conv_w, conv_b.reshape(1, f), wo)


def _mla_up_kernel(p_ref, cq_ref, sq_ref, ck_ref, sk_ref, qn_ref, kvn_ref,
                   wqa_ref, wqb_ref, wk_ref, wv_ref, pk_ref, q_ref, k_ref, v_ref, *, scale):
    r = MLA_Q_RANK
    cq = _rms(p_ref[0, :, 0:r], qn_ref[...]).astype(BF16)
    ckv = _rms(p_ref[0, :, r:2 * r], kvn_ref[...]).astype(BF16)
    kr = p_ref[0, :, 2 * r:2 * r + 128]
    krr = p_ref[0, :, 2 * r + 128:2 * r + 256]
    k_rope = (kr * ck_ref[...] + krr * sk_ref[...]).astype(BF16)
    k_ref[0] = (_dot(ckv, wk_ref[...]) + _dot(k_rope, pk_ref[...])).astype(BF16)
    v_ref[0] = _dot(ckv, wv_ref[...]).astype(BF16)
    qa = _dot(cq, wqa_ref[...])
    qb = _dot(cq, wqb_ref[...])
    cq_t = cq_ref[...] * scale
    sq_t = sq_ref[...] * scale
    for h in range(MLA_HEADS):
        sl = slice(h * MLA_HEAD_PAD, (h + 1) * MLA_HEAD_PAD)
        q_ref[0, :, sl] = (qa[:, sl] * cq_t + qb[:, sl] * sq_t).astype(BF16)


def _mla_up(p, tabs, q_norm, kv_norm, wqa, wqb, wk, wv, pk, tm):
    b, s, pw = p.shape
    cosq, sinq, cosk, sink = tabs
    hw = MLA_HEADS * MLA_HEAD_PAD
    vw = MLA_HEADS * MLA_V
    scale = (MLA_NOPE + MLA_ROPE) ** -0.5
    row = lambda w: pl.BlockSpec((tm, w), lambda i, t: (t, 0))
    return pl.pallas_call(
        functools.partial(_mla_up_kernel, scale=scale),
        grid=(b, s // tm),
        in_specs=[pl.BlockSpec((1, tm, pw), lambda i, t: (i, t, 0)),
                  row(MLA_HEAD_PAD), row(MLA_HEAD_PAD), row(128), row(128),
                  _const_spec((1, MLA_Q_RANK)), _const_spec((1, MLA_KV_RANK)),
                  _const_spec(wqa.shape), _const_spec(wqb.shape), _const_spec(wk.shape),
                  _const_spec(wv.shape), _const_spec(pk.shape)],
        out_specs=[pl.BlockSpec((1, tm, hw), lambda i, t: (i, t, 0)),
                   pl.BlockSpec((1, tm, hw), lambda i, t: (i, t, 0)),
                   pl.BlockSpec((1, tm, vw), lambda i, t: (i, t, 0))],
        out_shape=[jax.ShapeDtypeStruct((b, s, hw), BF16),
                   jax.ShapeDtypeStruct((b, s, hw), BF16),
                   jax.ShapeDtypeStruct((b, s, vw), BF16)],
        compiler_params=_params(("parallel", "parallel")),
        name="mla_up",
    )(p, cosq, sinq, cosk, sink, q_norm.reshape(1, -1), kv_norm.reshape(1, -1), wqa, wqb, wk, wv, pk)


def _softmax_pv(s, v):
    m = jnp.max(s, axis=-1, keepdims=True)
    p = jnp.exp(s - m)
    l = jnp.sum(p, axis=-1, keepdims=True)
    return _dot(p.astype(BF16), v) / l


def _mla_attn_kernel(q_ref, k_ref, v_ref, o_ref, *, ctx):
    t = pl.program_id(2)
    q = q_ref[0]

    @pl.when(t == 0)
    def _():
        o_ref[0] = _softmax_pv(_dot_nt(q, k_ref[0, 0:ctx, :]), v_ref[0, 0:ctx, :]).astype(o_ref.dtype)

    @pl.when(t > 0)
    def _():
        o_ref[0] = _softmax_pv(_dot_nt(q, k_ref[0]), v_ref[0]).astype(o_ref.dtype)


def _mla_attention(q, k, v, ctx, tq):
    b, s, _ = q.shape
    return pl.pallas_call(
        functools.partial(_mla_attn_kernel, ctx=ctx),
        grid=(b, MLA_HEADS, s // tq),
        in_specs=[pl.BlockSpec((1, tq, MLA_HEAD_PAD), lambda i, h, t: (i, t, h)),
                  pl.BlockSpec((1, s, MLA_HEAD_PAD), lambda i, h, t: (i, 0, h)),
                  pl.BlockSpec((1, s, MLA_V), lambda i, h, t: (i, 0, h))],
        out_specs=pl.BlockSpec((1, tq, MLA_V), lambda i, h, t: (i, t, h)),
        out_shape=jax.ShapeDtypeStruct((b, s, MLA_HEADS * MLA_V), BF16),
        compiler_params=_params(("parallel", "parallel", "parallel")),
        name="mla_attention",
    )(q, k, v)


def _na_kernel(q_ref, k_ref, v_ref, bias_ref, o_ref, *, ctx, rows):
    w = GRID_W
    win = NA_KH * w
    lane = lax.broadcasted_iota(jnp.int32, (w, 2 * NA_DH), 1)
    first = lane < NA_DH
    kc = k_ref[0, 0:ctx, :]
    vc = v_ref[0, 0:ctx, :]

    def split_heads(qr):
        zero = jnp.zeros_like(qr)
        return jnp.concatenate([jnp.where(first, qr, zero), jnp.where(first, zero, qr)], axis=0)

    def merge_heads(o2):
        n = o2.shape[0] // 2
        lane_n = lax.broadcasted_iota(jnp.int32, (n, 2 * NA_DH), 1)
        return jnp.where(lane_n < NA_DH, o2[:n], o2[n:])

    qc = q_ref[0, 0:ctx, :]
    zc = jnp.zeros_like(qc)
    lane_c = lax.broadcasted_iota(jnp.int32, (ctx, 2 * NA_DH), 1)
    qc2 = jnp.concatenate([jnp.where(lane_c < NA_DH, qc, zc), jnp.where(lane_c < NA_DH, zc, qc)], axis=0)
    o_ref[0, 0:ctx, :] = merge_heads(_softmax_pv(_dot_nt(qc2, kc), vc)).astype(o_ref.dtype)

    def one_row(r, carry):
        r_start = jnp.clip(r - NA_KH // 2, 0, rows - NA_KH)
        pat = jnp.where(r < NA_KH // 2, r,
                        jnp.where(r <= rows - NA_KH // 2, NA_KH // 2, r - (rows - NA_KH)))
        q0 = pl.multiple_of(ctx + r * w, w)
        k0 = pl.multiple_of(ctx + r_start * w, w)
        q2 = split_heads(q_ref[0, pl.ds(q0, w), :])
        kw = k_ref[0, pl.ds(k0, win), :]
        vw = v_ref[0, pl.ds(k0, win), :]
        s_lat = _dot_nt(q2, kw) + bias_ref[0, pat]
        s_ctx = _dot_nt(q2, kc)
        m = jnp.maximum(jnp.max(s_lat, axis=-1, keepdims=True), jnp.max(s_ctx, axis=-1, keepdims=True))
        p_lat = jnp.exp(s_lat - m)
        p_ctx = jnp.exp(s_ctx - m)
        l = jnp.sum(p_lat, axis=-1, keepdims=True) + jnp.sum(p_ctx, axis=-1, keepdims=True)
        o2 = (_dot(p_lat.astype(BF16), vw) + _dot(p_ctx.astype(BF16), vc)) / l
        o_ref[0, pl.ds(q0, w), :] = merge_heads(o2).astype(o_ref.dtype)
        return carry

    lax.fori_loop(0, rows, one_row, 0)


def _na_attention(q, k, v, bias, ctx):
    b, s, width = q.shape
    rows = (s - ctx) // GRID_W
    pairs = width // (2 * NA_DH)
    blk = pl.BlockSpec((1, s, 2 * NA_DH), lambda i, h: (i, 0, h))
    return pl.pallas_call(
        functools.partial(_na_kernel, ctx=ctx, rows=rows),
        grid=(b, pairs),
        in_specs=[blk, blk, blk,
                  pl.BlockSpec((1, NA_PATTERNS, 2 * GRID_W, NA_KH * GRID_W), lambda i, h: (h, 0, 0, 0))],
        out_specs=blk,
        out_shape=jax.ShapeDtypeStruct((b, s, width), BF16),
        compiler_params=_params(("parallel", "parallel")),
        name="na_attention",
    )(q, k, v, bias)


def _na_bias_kernel(rpb_ref, o_ref, *, rows):
    w, kh, kw = GRID_W, NA_KH, NA_KW
    pair = pl.program_id(0)
    qi = lax.broadcasted_iota(jnp.int32, (w, w), 0)
    ki = lax.broadcasted_iota(jnp.int32, (w, w), 1)
    col_idx = jnp.clip(ki - qi, -(kw - 1), kw - 1) + kw - 1
    c_start = jnp.clip(qi - kw // 2, 0, w - kw)
    masked = jnp.where((ki >= c_start) & (ki < c_start + kw), 0.0, NEG_BIG)
    rep = (0, 1, 2, 3, 4, rows - 3, rows - 2, rows - 1)
    for hh in range(2):
        head = 2 * pair + hh
        tiles = []
        for dr in range(2 * kh - 1):
            t = masked
            for dc in range(2 * kw - 1):
                t = jnp.where(col_idx == dc, masked + rpb_ref[head, dr, dc], t)
            tiles.append(t)
        for p, r in enumerate(rep):
            r_start = min(max(r - kh // 2, 0), rows - kh)
            for j in range(kh):
                o_ref[0, p, hh * w:(hh + 1) * w, j * w:(j + 1) * w] = tiles[r_start + j - r + kh - 1]


def _na_bias_table(rpb, rows):
    h = rpb.shape[0]
    shape = (h // 2, NA_PATTERNS, 2 * GRID_W, NA_KH * GRID_W)
    return pl.pallas_call(
        functools.partial(_na_bias_kernel, rows=rows),
        grid=(h // 2,),
        in_specs=[pl.BlockSpec(memory_space=pltpu.SMEM)],
        out_specs=pl.BlockSpec((1,) + shape[1:], lambda i: (i, 0, 0, 0)),
        out_shape=jax.ShapeDtypeStruct(shape, F32),
        compiler_params=_params(("parallel",)),
        name="na_bias_table",
    )(rpb.astype(F32))


def _gdn_kernel(q_ref, k_ref, v_ref, z_ref, ba_ref, cwq_ref, cwk_ref, cwv_ref, alog_ref, dtb_ref, gain_ref,
                o_ref, qs, ks, vs, wq_s, ut_s, qkk_s, gl_s, o_s, *, ctx, seq):
    c = GDN_CHUNK
    dk = GDN_DK
    grp = GDN_GROUP
    gr = grp * c
    nc = seq // c
    ncc = ctx // c
    ng = seq // gr
    head = pl.program_id(1)

    row = lax.broadcasted_iota(jnp.int32, (seq, 128), 0)
    pad = SHORT_CONV // 2

    def short_conv(x_ref, cw_ref):
        x = x_ref[0]
        acc = x * cw_ref[pad:pad + 1, :]
        for j in range(SHORT_CONV):
            d = j - pad
            if d == 0:
                continue
            src = row + d
            ok = (src >= 0) & (src < seq) & ((row < ctx) == (src < ctx))
            xs = pltpu.roll(x, (-d) % seq, axis=0)
            acc = acc + jnp.where(ok, xs, 0.0) * cw_ref[j:j + 1, :]
        return _silu(acc)

    def l2n(x):
        return x * lax.rsqrt(jnp.sum(x * x, axis=-1, keepdims=True) + EPS)

    qs[...] = l2n(short_conv(q_ref, cwq_ref)) * (dk ** -0.5)
    ks[...] = l2n(short_conv(k_ref, cwk_ref))
    vs[...] = short_conv(v_ref, cwv_ref)

    ii = lax.broadcasted_iota(jnp.int32, (c, gr), 0)
    lane_cat = lax.broadcasted_iota(jnp.int32, (c, gr), 1)
    jj = lane_cat & (c - 1)
    lblk = lane_cat >> 6
    eye_cat = (ii == jj).astype(F32)
    same8 = (ii >> 3) == (jj >> 3)
    bdmask = ((lax.broadcasted_iota(jnp.int32, (gr, gr), 0) >> 6)
              == (lax.broadcasted_iota(jnp.int32, (gr, gr), 1) >> 6))

    def bd(y):
        return jnp.where(bdmask, jnp.concatenate([y] * grp, axis=0), 0.0).astype(BF16)

    def mm(x, ybd):
        return _dot(x.astype(BF16), ybd)

    def diag_cat(f):
        out = jnp.where(lblk == 0, f[0:c], 0.0)
        for b in range(1, grp):
            out = jnp.where(lblk == b, f[b * c:(b + 1) * c], out)
        return out

    def inverse_cat(a):
        p = jnp.where(same8, -a, 0.0)
        t = eye_cat + p
        p2 = mm(p, bd(p))
        r = mm(jnp.concatenate([t, p2], axis=0), bd(p2))
        t = t + r[0:c]
        t = t + mm(t, bd(r[c:2 * c]))
        for sh in (3, 4, 5):
            off = ((ii >> (sh + 1)) == (jj >> (sh + 1))) & ((ii >> sh) != (jj >> sh))
            l_mat = jnp.where(off, a, 0.0)
            t = t - mm(t, bd(mm(l_mat, bd(t))))
        return t

    lane_g = lax.broadcasted_iota(jnp.int32, (gr, 128), 1)
    row_g = lax.broadcasted_iota(jnp.int32, (gr, 128), 0)
    pos = row_g & (c - 1)
    lane_b = ((lane_g >= 6).astype(jnp.int32) + (lane_g >= 12).astype(jnp.int32)
              + (lane_g >= 18).astype(jnp.int32))
    lane_t = lane_g - 6 * lane_b
    active = (lane_g < 6 * grp) & (lane_b == (row_g >> 6))

    def split3(x):
        hi = x.astype(BF16).astype(F32)
        mid = (x - hi).astype(BF16).astype(F32)
        return hi, mid, x - hi - mid

    def row_features(x):
        hi, mid, lo = split3(x)
        f = jnp.where(lane_t == 0, hi, jnp.where(lane_t == 1, mid, jnp.where(lane_t == 2, lo, 1.0)))
        f = jnp.where(active, f, 0.0)
        acc = f[0:c]
        for b in range(1, grp):
            acc = acc + f[b * c:(b + 1) * c]
        return acc.astype(BF16)

    def col_features(x):
        hi, mid, lo = split3(x)
        f = jnp.where(lane_t < 3, 1.0, jnp.where(lane_t == 3, -hi, jnp.where(lane_t == 4, -mid, -lo)))
        return jnp.where(active, f, 0.0).astype(BF16)

    def prefix(x):
        for s in (1, 2, 4, 8, 16, 32):
            x = x + jnp.where(pos >= s, pltpu.roll(x, s, axis=0), 0.0)
        return x

    def suffix(x):
        for s in (1, 2, 4, 8, 16, 32):
            x = x + jnp.where(pos < c - s, pltpu.roll(x, gr - s, axis=0), 0.0)
        return x

    def pick(x, lane_id):
        col = jnp.sum(jnp.where(lane_g == lane_id, x, 0.0), axis=-1, keepdims=True)
        return jnp.broadcast_to(col, (gr, 128))

    def prep(gi, carry):
        r0 = pl.multiple_of(gi * gr, gr)
        q = qs[pl.ds(r0, gr), :]
        k = ks[pl.ds(r0, gr), :]
        v = vs[pl.ds(r0, gr), :]
        ba = ba_ref[0, pl.ds(r0, gr), :]
        beta_all = jax.nn.sigmoid(ba)
        logb_all = -_softplus(-ba)
        g_all = -jnp.exp(alog_ref[...]) * _softplus(ba + dtb_ref[...])
        kb = k.astype(BF16)
        kk_cat = diag_cat(_dot_nt(kb, kb))
        qk_cat = diag_cat(_dot_nt(q.astype(BF16), kb))
        for d in range(2):
            beta = pick(beta_all, d * GDN_HEADS + head)
            logb = pick(logb_all, d * GDN_HEADS + head)
            g = pick(g_all, 2 * GDN_HEADS + d * GDN_HEADS + head)
            if d == 0:
                gc, grest = prefix(g), suffix(g) - g
                incl, strict = ii >= jj, ii > jj
            else:
                gc, grest = suffix(g), prefix(g) - g
                incl, strict = ii <= jj, ii < jj
            cf = col_features(gc)
            diff = _dot_nt(row_features(gc), cf)
            diffb = _dot_nt(row_features(gc + logb), cf)
            decay = jnp.where(incl, jnp.exp(jnp.minimum(diff, 0.0)), 0.0)
            a_cat = jnp.where(strict, kk_cat * jnp.exp(jnp.minimum(diffb, 0.0)), 0.0)
            t_cat = inverse_cat(a_cat)
            qkd = qk_cat * decay
            eg = jnp.exp(gc)
            rhs = jnp.concatenate([(k * (beta * eg)).astype(BF16), (v * beta).astype(BF16)], axis=1)
            qd = (q * eg).astype(BF16)
            kd = k * jnp.exp(grest)
            for b in range(grp):
                n = gi * grp + b
                rows = slice(b * c, (b + 1) * c)
                sol = _dot(jnp.where(lblk == b, t_cat, 0.0).astype(BF16), rhs)
                wq_s[d, n] = jnp.concatenate([sol[:, 0:dk].astype(BF16), qd[rows]], axis=0)
                ut_s[d, n] = sol[:, dk:]
                qkk_s[d, n] = jnp.concatenate([qkd[:, rows], kd[rows].T], axis=0).astype(BF16)
                edge = b * c + c - 1 if d == 0 else b * c
                gl_s[d, n] = jnp.broadcast_to(eg[edge:edge + 1, :], (8, 128))
        return carry

    lax.fori_loop(0, ng, prep, 0)

    def step(i, states):
        n_bwd = jnp.where(i < ncc, ncc - 1 - i, nc - 1 - (i - ncc))
        new = []
        for d, n in ((0, i), (1, n_bwd)):
            state = states[d]
            sb = state.astype(BF16)
            r1 = _dot(wq_s[d, n], sb)
            ub = (ut_s[d, n] - r1[0:c]).astype(BF16)
            r2 = _dot(qkk_s[d, n], ub)
            o_s[d, pl.ds(pl.multiple_of(n * c, c), c), :] = r1[c:2 * c] + r2[0:c]
            new.append(gl_s[d, n][0:1, :] * state + r2[c:c + dk])
        return tuple(new)

    zero = jnp.zeros((dk, GDN_DV), F32)
    lax.fori_loop(0, nc, step, (zero, zero))

    o = _rms(o_s[0] + o_s[1], gain_ref[...]) * _silu(z_ref[0])
    o_ref[0] = o.astype(o_ref.dtype)


def _gdn(qkv, z, ba, conv_w, alog_row, dtb_row, gain, ctx):
    b, s, _ = qkv.shape
    h = GDN_HEADS
    c = GDN_CHUNK
    assert s % (GDN_GROUP * c) == 0
    nc = s // c
    col = lambda off: pl.BlockSpec((1, s, 128), lambda i, j: (i, 0, off + j))
    cw = lambda off: pl.BlockSpec((SHORT_CONV, 128), lambda i, j: (0, off + j))
    scratch = [pltpu.VMEM((s, 128), F32), pltpu.VMEM((s, 128), F32), pltpu.VMEM((s, 128), F32),
               pltpu.VMEM((2, nc, 2 * c, 128), BF16), pltpu.VMEM((2, nc, c, 128), F32),
               pltpu.VMEM((2, nc, c + GDN_DK, c), BF16), pltpu.VMEM((2, nc, 8, 128), F32),
               pltpu.VMEM((2, s, 128), F32)]
    return pl.pallas_call(
        functools.partial(_gdn_kernel, ctx=ctx, seq=s),
        grid=(b, h),
        in_specs=[col(0), col(h), col(2 * h),
                  pl.BlockSpec((1, s, 128), lambda i, j: (i, 0, j)),
                  pl.BlockSpec((1, s, 128), lambda i, j: (i, 0, 0)),
                  cw(0), cw(h), cw(2 * h),
                  _const_spec((1, 128)), _const_spec((1, 128)), _const_spec((1, 128))],
        out_specs=pl.BlockSpec((1, s, 128), lambda i, j: (i, 0, j)),
        out_shape=jax.ShapeDtypeStruct((b, s, h * GDN_DV), BF16),
        scratch_shapes=scratch,
        compiler_params=_params(("parallel", "parallel")),
        name="gdn",
    )(qkv, qkv, qkv, z, ba, conv_w, conv_w, conv_w, alog_row, dtb_row, gain.reshape(1, -1))


def _final_kernel(x_ref, g_ref, o_ref):
    o_ref[0] = _rms(x_ref[0], g_ref[...])


def _final_norm(xa, g, ctx, tm):
    b, s, d = xa.shape
    off = ctx // tm
    return pl.pallas_call(
        _final_kernel,
        grid=(b, (s - ctx) // tm),
        in_specs=[pl.BlockSpec((1, tm, d), lambda i, t: (i, t + off, 0)), _const_spec((1, d))],
        out_specs=pl.BlockSpec((1, tm, d), lambda i, t: (i, t, 0)),
        out_shape=jax.ShapeDtypeStruct((b, s - ctx, d), F32),
        compiler_params=_params(("parallel", "parallel")),
        name="final_norm",
    )(xa, g.reshape(1, d))


def _rope_tables(n_lat, ctx):
    t = jnp.arange(n_lat)
    row = (t // GRID_W).astype(F32)
    col = (t % GRID_W).astype(F32)
    quarter = MLA_ROPE // 4
    inv_freq = ROPE_BASE ** (-jnp.arange(quarter, dtype=F32) / quarter)
    ang_r = row[:, None] * inv_freq
    ang_c = col[:, None] * inv_freq
    ang = jnp.concatenate([ang_r, ang_r, ang_c, ang_c], axis=-1)
    cos = jnp.concatenate([jnp.ones((ctx, MLA_ROPE), F32), jnp.cos(ang)], axis=0)
    sin = jnp.concatenate([jnp.zeros((ctx, MLA_ROPE), F32), jnp.sin(ang)], axis=0)
    s = ctx + n_lat
    one = jnp.ones((s, MLA_NOPE), F32)
    zn = jnp.zeros((s, MLA_NOPE), F32)
    zp = jnp.zeros((s, MLA_HEAD_PAD - MLA_NOPE - MLA_ROPE), F32)
    cosq = jnp.concatenate([one, cos, zp], axis=1)
    sinq = jnp.concatenate([zn, sin, zp], axis=1)
    zk = jnp.zeros((s, 128 - MLA_ROPE), F32)
    cosk = jnp.concatenate([cos, zk], axis=1)
    sink = jnp.concatenate([sin, zk], axis=1)
    return cosq, sinq, cosk, sink


def _rot_cols(w):
    q = MLA_ROPE // 4
    a, b, c, d = w[..., 0:q], w[..., q:2 * q], w[..., 2 * q:3 * q], w[..., 3 * q:4 * q]
    return jnp.concatenate([-b, a, -d, c], axis=-1)


def _even_weights(w_in, w_uq, w_ukv, w_out):
    d = w_in.shape[0]
    h = GDN_HEADS
    qkv_w = 2 * h * GDN_DK + h * GDN_DV
    vw = h * GDN_DV
    o = 0
    w_qkv = w_in[:, o:o + qkv_w]; o += qkv_w
    w_z = w_in[:, o:o + vw]; o += vw
    w_ba = w_in[:, o:o + 4 * h]; o += 4 * h
    w_cq = w_in[:, o:o + MLA_Q_RANK]; o += MLA_Q_RANK
    w_ckv = w_in[:, o:o + MLA_KV_RANK]; o += MLA_KV_RANK
    w_kr = w_in[:, o:o + MLA_ROPE]
    z64 = jnp.zeros((d, 128 - MLA_ROPE), w_in.dtype)
    w_ba = jnp.concatenate([w_ba, jnp.zeros((d, 128 - 4 * h), w_in.dtype)], axis=1)
    w_mla = jnp.concatenate([w_cq, w_ckv, w_kr, z64, _rot_cols(w_kr), z64], axis=1)
    proj = [w_qkv.astype(BF16), w_z.astype(BF16), w_ba.astype(BF16), w_mla.astype(BF16)]

    rq = MLA_Q_RANK
    wq = w_uq.reshape(rq, MLA_HEADS, MLA_NOPE + MLA_ROPE)
    zpad = jnp.zeros((rq, MLA_HEADS, MLA_HEAD_PAD - MLA_NOPE - MLA_ROPE), w_uq.dtype)
    znope = jnp.zeros((rq, MLA_HEADS, MLA_NOPE), w_uq.dtype)
    wqa = jnp.concatenate([wq, zpad], axis=-1).reshape(rq, -1)
    wqb = jnp.concatenate([znope, _rot_cols(wq[..., MLA_NOPE:]), zpad], axis=-1).reshape(rq, -1)
    rk = MLA_KV_RANK
    wkv = w_ukv.reshape(rk, MLA_HEADS, MLA_NOPE + MLA_V)
    wk = jnp.concatenate([wkv[..., :MLA_NOPE], jnp.zeros((rk, MLA_HEADS, MLA_HEAD_PAD - MLA_NOPE), w_ukv.dtype)],
                         axis=-1).reshape(rk, -1)
    wv = wkv[..., MLA_NOPE:].reshape(rk, -1)
    eye = jnp.eye(MLA_ROPE, dtype=F32)
    blk = jnp.concatenate([jnp.zeros((MLA_ROPE, MLA_NOPE), F32), eye,
                           jnp.zeros((MLA_ROPE, MLA_HEAD_PAD - MLA_NOPE - MLA_ROPE), F32)], axis=1)
    pk = jnp.concatenate([jnp.tile(blk, (1, MLA_HEADS)),
                          jnp.zeros((128 - MLA_ROPE, MLA_HEADS * MLA_HEAD_PAD), F32)], axis=0)
    up = [wqa.astype(BF16), wqb.astype(BF16), wk.astype(BF16), wv.astype(BF16), pk.astype(BF16)]
    out = [w_out[:vw].astype(BF16), w_out[vw:].astype(BF16)]
    return proj, up, out


def _gate_rows(a_log, dt_bias):
    n = 2 * GDN_HEADS
    z = jnp.zeros((n,), F32)
    tail = jnp.zeros((128 - 2 * n,), F32)
    alog_row = jnp.concatenate([z, a_log.astype(F32).reshape(n), tail]).reshape(1, 128)
    dtb_row = jnp.concatenate([z, dt_bias.astype(F32).reshape(n), tail]).reshape(1, 128)
    return alog_row, dtb_row


def kernel(x, c, ctx, c_ctx, ada_w, ada_b, norm_mix_g, norm_ffn_g, ev_w_in, ev_conv_qkv, ev_a_log, ev_dt_bias,
           ev_gdn_norm, ev_q_norm, ev_kv_norm, ev_w_uq, ev_w_ukv, ev_w_out, od_w_qkv, od_rpb, od_w_out,
           ffn_w_in, ffn_conv, ffn_conv_b, ffn_w_out, final_g):
    b, n_lat, d = x.shape
    n_ctx = ctx.shape[1]
    depth = ada_w.shape[0]
    tm = n_ctx
    assert n_lat % tm == 0 and n_lat % GRID_W == 0 and tm % GDN_CHUNK == 0
    rows = n_lat // GRID_W
    assert rows >= NA_KH

    r_pad = -(b + 1) % 8
    cond = jnp.concatenate([c, c_ctx[None, :], jnp.zeros((r_pad, d), c.dtype)], axis=0)
    mods = _ada_table(cond, ada_w, ada_b).reshape(depth, b + 1 + r_pad, 6, d)
    tabs = _rope_tables(n_lat, n_ctx)
    d_ff = ffn_w_out.shape[1]

    xa = jnp.concatenate([ctx, x], axis=1)
    for i in range(depth):
        j = i // 2
        m = mods[i]
        if i % 2 == 0:
            proj_w, up_w, out_w = _even_weights(ev_w_in[j], ev_w_uq[j], ev_w_ukv[j], ev_w_out[j])
            qkv, z, ba, p_mla = _norm_mod_matmul(xa, m, norm_mix_g[i], proj_w, [F32, F32, F32, F32], 0, tm)
            alog_row, dtb_row = _gate_rows(ev_a_log[j], ev_dt_bias[j])
            y_a = _gdn(qkv, z, ba, ev_conv_qkv[j], alog_row, dtb_row, ev_gdn_norm[j], n_ctx)
            q, k, v = _mla_up(p_mla, tabs, ev_q_norm[j], ev_kv_norm[j], *up_w, tm)
            y_b = _mla_attention(q, k, v, n_ctx, tm)
            xa = _outproj_residual(xa, m, [y_a, y_b], out_w, 2, tm)
        else:
            w = od_w_qkv[j]
            nw = w.shape[1] // 3
            ws = [(w[:, :nw] * (NA_DH ** -0.5)).astype(BF16), w[:, nw:2 * nw].astype(BF16),
                  w[:, 2 * nw:].astype(BF16)]
            q, k, v = _norm_mod_matmul(xa, m, norm_mix_g[i], ws, [BF16, BF16, BF16], 0, tm)
            y = _na_attention(q, k, v, _na_bias_table(od_rpb[j], rows), n_ctx)
            xa = _outproj_residual(xa, m, [y], [od_w_out[j].astype(BF16)], 2, tm)
        wi = ffn_w_in[i]
        xa = _conv_ffn_residual(xa, m, norm_ffn_g[i], wi[:, :d_ff].astype(BF16), wi[:, d_ff:].astype(BF16),
                                ffn_conv[i], ffn_conv_b[i], ffn_w_out[i].astype(BF16), tm)
    return _final_norm(xa, final_g, n_ctx, tm)
```

```python
import functools
import math

import jax
import jax.numpy as jnp
from jax import lax
from jax.experimental import pallas as pl
from jax.experimental.pallas import tpu as pltpu

EPS = 1e-6
ROPE_BASE = 10000.0
GRID_W = 64

GDN_HEADS = 4
GDN_DK = 128
GDN_DV = 128
GDN_CHUNK = 64
GDN_GROUP = 4
GDN_HEADS_PER_STEP = 2
SHORT_CONV = 5

MLA_HEADS = 4
MLA_NOPE = 128
MLA_ROPE = 64
MLA_V = 128
MLA_Q_RANK = 256
MLA_KV_RANK = 256
MLA_HEAD_PAD = 256

NA_HEADS = 16
NA_DH = 64
NA_KH = 8
NA_KW = 16
NA_PATTERNS = 8

FFN_CONV = 3
HALO = 8

VMEM_LIMIT = 56 * 1024 * 1024
NEG_BIG = -1e30

BF16 = jnp.bfloat16
F32 = jnp.float32


def _dot(a, b):
    return jnp.dot(a, b, preferred_element_type=F32)


def _dot_nt(a, b):
    return lax.dot_general(a, b, (((1,), (1,)), ((), ())), preferred_element_type=F32)


def _silu(x):
    return x * jax.nn.sigmoid(x)


def _softplus(x):
    return jnp.maximum(x, 0.0) + jnp.log1p(jnp.exp(-jnp.abs(x)))


def _rms(x, g):
    return x * lax.rsqrt(jnp.mean(x * x, axis=-1, keepdims=True) + EPS) * g


def _params(sem):
    return pltpu.CompilerParams(dimension_semantics=sem, vmem_limit_bytes=VMEM_LIMIT)


def _const_spec(shape):
    nd = len(shape)
    return pl.BlockSpec(shape, lambda *_: (0,) * nd)


def _ada_kernel(c_ref, w_ref, b_ref, o_ref):
    s = _silu(c_ref[...]).astype(BF16)
    o_ref[0] = _dot(s, w_ref[0]) + b_ref[0]


def _ada_table(cond, ada_w, ada_b):
    depth, d, n = ada_w.shape
    r = cond.shape[0]
    tn = 1536
    return pl.pallas_call(
        _ada_kernel,
        grid=(depth, n // tn),
        in_specs=[pl.BlockSpec((r, d), lambda i, j: (0, 0)),
                  pl.BlockSpec((1, d, tn), lambda i, j: (i, 0, j)),
                  pl.BlockSpec((1, 1, tn), lambda i, j: (i, 0, j))],
        out_specs=pl.BlockSpec((1, r, tn), lambda i, j: (i, 0, j)),
        out_shape=jax.ShapeDtypeStruct((depth, r, n), F32),
        compiler_params=_params(("parallel", "parallel")),
        name="ada_table",
    )(cond, ada_w.astype(BF16), ada_b.reshape(depth, 1, n))


def _nmm_kernel(x_ref, mod_ref, g_ref, *rest, n_out, shift_idx):
    w_refs, o_refs = rest[:n_out], rest[n_out:]
    shift = mod_ref[0, shift_idx:shift_idx + 1, :]
    scale = mod_ref[0, shift_idx + 1:shift_idx + 2, :]
    h = _rms(x_ref[0], g_ref[...]) * (1.0 + scale) + shift
    hb = h.astype(BF16)
    for w_ref, o_ref in zip(w_refs, o_refs):
        o_ref[0] = _dot(hb, w_ref[...]).astype(o_ref.dtype)


def _norm_mod_matmul(xa, mods, g, ws, out_dtypes, shift_idx, tm):
    b, s, d = xa.shape
    n_out = len(ws)
    in_specs = [pl.BlockSpec((1, tm, d), lambda i, t: (i, t, 0)),
                pl.BlockSpec((1, 6, d), lambda i, t: (jnp.where(t == 0, b, i), 0, 0)),
                _const_spec((1, d))]
    in_specs += [_const_spec(w.shape) for w in ws]
    out_specs = [pl.BlockSpec((1, tm, w.shape[1]), lambda i, t: (i, t, 0)) for w in ws]
    out_shape = [jax.ShapeDtypeStruct((b, s, w.shape[1]), dt) for w, dt in zip(ws, out_dtypes)]
    return pl.pallas_call(
        functools.partial(_nmm_kernel, n_out=n_out, shift_idx=shift_idx),
        grid=(b, s // tm),
        in_specs=in_specs, out_specs=out_specs, out_shape=out_shape,
        compiler_params=_params(("parallel", "parallel")),
        name="norm_mod_matmul",
    )(xa, mods, g.reshape(1, d), *ws)


def _outproj_kernel(x_ref, mod_ref, *rest, n_in, gate_idx):
    y_refs, w_refs, o_ref = rest[:n_in], rest[n_in:2 * n_in], rest[2 * n_in]
    acc = _dot(y_refs[0][0], w_refs[0][...])
    for y_ref, w_ref in zip(y_refs[1:], w_refs[1:]):
        acc = acc + _dot(y_ref[0], w_ref[...])
    gate = mod_ref[0, gate_idx:gate_idx + 1, :]
    o_ref[0] = x_ref[0] + gate * acc


def _outproj_residual(xa, mods, ys, ws, gate_idx, tm):
    b, s, d = xa.shape
    n_in = len(ys)
    in_specs = [pl.BlockSpec((1, tm, d), lambda i, t: (i, t, 0)),
                pl.BlockSpec((1, 6, d), lambda i, t: (jnp.where(t == 0, b, i), 0, 0))]
    in_specs += [pl.BlockSpec((1, tm, y.shape[2]), lambda i, t: (i, t, 0)) for y in ys]
    in_specs += [_const_spec(w.shape) for w in ws]
    return pl.pallas_call(
        functools.partial(_outproj_kernel, n_in=n_in, gate_idx=gate_idx),
        grid=(b, s // tm),
        in_specs=in_specs,
        out_specs=pl.BlockSpec((1, tm, d), lambda i, t: (i, t, 0)),
        out_shape=jax.ShapeDtypeStruct((b, s, d), F32),
        compiler_params=_params(("parallel", "parallel")),
        name="outproj_residual",
    )(xa, mods, *ys, *ws)


def _ffn_kernel(x_ref, xp_ref, xn_ref, mod_ref, g_ref, wu_ref, wg_ref, cw_ref, cb_ref, wo_ref, fg_ref, o_ref,
                *, tm, n_tiles, t_off, final):
    t = pl.program_id(1) + t_off
    shift = mod_ref[0, 3:4, :]
    scale = mod_ref[0, 4:5, :]
    gate = mod_ref[0, 5:6, :]
    x = x_ref[0]
    x_ext = jnp.concatenate([xp_ref[0], x, xn_ref[0]], axis=0)
    h_ext = (_rms(x_ext, g_ref[...]) * (1.0 + scale) + shift).astype(BF16)
    gt = _dot(h_ext, wg_ref[...])
    u = _dot(h_ext[HALO:HALO + tm], wu_ref[...])
    prev_ok = (t >= 2).astype(F32)
    next_ok = jnp.logical_and(t >= 1, t < n_tiles - 1).astype(F32)
    row = lax.broadcasted_iota(jnp.int32, (tm + 2 * HALO, 1), 0)
    keep = jnp.where(row < HALO, prev_ok, jnp.where(row >= tm + HALO, next_ok, 1.0))
    gt = gt * keep
    g_prev = pltpu.roll(gt, 1, axis=0)[HALO:HALO + tm]
    g_next = pltpu.roll(gt, tm + 2 * HALO - 1, axis=0)[HALO:HALO + tm]
    g_mid = gt[HALO:HALO + tm]
    conv = g_prev * cw_ref[0:1, :] + g_mid * cw_ref[1:2, :] + g_next * cw_ref[2:3, :] + cb_ref[...]
    act = (_silu(conv) * u).astype(BF16)
    y = x + gate * _dot(act, wo_ref[...])
    o_ref[0] = _rms(y, fg_ref[...]) if final else y


def _conv_ffn_residual(xa, mods, g, wu, wg, conv_w, conv_b, wo, tm, final_g=None):
    b, s, d = xa.shape
    f = wu.shape[1]
    n_tiles = s // tm
    hb = tm // HALO
    last = s // HALO - 1
    final = final_g is not None
    t_off = 1 if final else 0
    fg = final_g if final else g
    mod_row = (lambda i, t: (i, 0, 0)) if final else (lambda i, t: (jnp.where(t == 0, b, i), 0, 0))
    return pl.pallas_call(
        functools.partial(_ffn_kernel, tm=tm, n_tiles=n_tiles, t_off=t_off, final=final),
        grid=(b, n_tiles - t_off),
        in_specs=[pl.BlockSpec((1, tm, d), lambda i, t: (i, t + t_off, 0)),
                  pl.BlockSpec((1, HALO, d), lambda i, t: (i, jnp.maximum((t + t_off) * hb - 1, 0), 0)),
                  pl.BlockSpec((1, HALO, d), lambda i, t: (i, jnp.minimum((t + t_off + 1) * hb, last), 0)),
                  pl.BlockSpec((1, 6, d), mod_row),
                  _const_spec((1, d)),
                  _const_spec((d, f)), _const_spec((d, f)),
                  _const_spec((FFN_CONV, f)), _const_spec((1, f)),
                  _const_spec((f, d)), _const_spec((1, d))],
        out_specs=pl.BlockSpec((1, tm, d), lambda i, t: (i, t, 0)),
        out_shape=jax.ShapeDtypeStruct((b, s - t_off * tm, d), F32),
        compiler_params=_params(("parallel", "parallel")),
        name="conv_ffn",
    )(xa, xa, xa, mods, g.reshape(1, d), wu, wg, conv_w, conv_b.reshape(1, f), wo, fg.reshape(1, d))


def _mla_up_kernel(p_ref, cq_ref, sq_ref, ck_ref, sk_ref, qn_ref, kvn_ref,
                   wqa_ref, wqb_ref, wk_ref, wv_ref, pk_ref, q_ref, k_ref, v_ref, *, scale):
    r = MLA_Q_RANK
    cq = _rms(p_ref[0, :, 0:r], qn_ref[...]).astype(BF16)
    ckv = _rms(p_ref[0, :, r:2 * r], kvn_ref[...]).astype(BF16)
    kr = p_ref[0, :, 2 * r:2 * r + 128]
    krr = p_ref[0, :, 2 * r + 128:2 * r + 256]
    k_rope = (kr * ck_ref[...] + krr * sk_ref[...]).astype(BF16)
    k_ref[0] = (_dot(ckv, wk_ref[...]) + _dot(k_rope, pk_ref[...])).astype(BF16)
    v_ref[0] = _dot(ckv, wv_ref[...]).astype(BF16)
    qa = _dot(cq, wqa_ref[...])
    qb = _dot(cq, wqb_ref[...])
    cq_t = cq_ref[...] * scale
    sq_t = sq_ref[...] * scale
    for h in range(MLA_HEADS):
        sl = slice(h * MLA_HEAD_PAD, (h + 1) * MLA_HEAD_PAD)
        q_ref[0, :, sl] = (qa[:, sl] * cq_t + qb[:, sl] * sq_t).astype(BF16)


def _mla_up(p, tabs, q_norm, kv_norm, wqa, wqb, wk, wv, pk, tm):
    b, s, pw = p.shape
    cosq, sinq, cosk, sink = tabs
    hw = MLA_HEADS * MLA_HEAD_PAD
    vw = MLA_HEADS * MLA_V
    scale = (MLA_NOPE + MLA_ROPE) ** -0.5
    row = lambda w: pl.BlockSpec((tm, w), lambda i, t: (t, 0))
    return pl.pallas_call(
        functools.partial(_mla_up_kernel, scale=scale),
        grid=(b, s // tm),
        in_specs=[pl.BlockSpec((1, tm, pw), lambda i, t: (i, t, 0)),
                  row(MLA_HEAD_PAD), row(MLA_HEAD_PAD), row(128), row(128),
                  _const_spec((1, MLA_Q_RANK)), _const_spec((1, MLA_KV_RANK)),
                  _const_spec(wqa.shape), _const_spec(wqb.shape), _const_spec(wk.shape),
                  _const_spec(wv.shape), _const_spec(pk.shape)],
        out_specs=[pl.BlockSpec((1, tm, hw), lambda i, t: (i, t, 0)),
                   pl.BlockSpec((1, tm, hw), lambda i, t: (i, t, 0)),
                   pl.BlockSpec((1, tm, vw), lambda i, t: (i, t, 0))],
        out_shape=[jax.ShapeDtypeStruct((b, s, hw), BF16),
                   jax.ShapeDtypeStruct((b, s, hw), BF16),
                   jax.ShapeDtypeStruct((b, s, vw), BF16)],
        compiler_params=_params(("parallel", "parallel")),
        name="mla_up",
    )(p, cosq, sinq, cosk, sink, q_norm.reshape(1, -1), kv_norm.reshape(1, -1), wqa, wqb, wk, wv, pk)


def _softmax_pv(s, v):
    m = jnp.max(s, axis=-1, keepdims=True)
    p = jnp.exp(s - m)
    l = jnp.sum(p, axis=-1, keepdims=True)
    return _dot(p.astype(BF16), v) / l


def _mla_attn_kernel(q_ref, k_ref, v_ref, o_ref, *, ctx):
    t = pl.program_id(2)
    q = q_ref[0]

    @pl.when(t == 0)
    def _():
        o_ref[0] = _softmax_pv(_dot_nt(q, k_ref[0, 0:ctx, :]), v_ref[0, 0:ctx, :]).astype(o_ref.dtype)

    @pl.when(t > 0)
    def _():
        o_ref[0] = _softmax_pv(_dot_nt(q, k_ref[0]), v_ref[0]).astype(o_ref.dtype)


def _mla_attention(q, k, v, ctx, tq):
    b, s, _ = q.shape
    return pl.pallas_call(
        functools.partial(_mla_attn_kernel, ctx=ctx),
        grid=(b, MLA_HEADS, s // tq),
        in_specs=[pl.BlockSpec((1, tq, MLA_HEAD_PAD), lambda i, h, t: (i, t, h)),
                  pl.BlockSpec((1, s, MLA_HEAD_PAD), lambda i, h, t: (i, 0, h)),
                  pl.BlockSpec((1, s, MLA_V), lambda i, h, t: (i, 0, h))],
        out_specs=pl.BlockSpec((1, tq, MLA_V), lambda i, h, t: (i, t, h)),
        out_shape=jax.ShapeDtypeStruct((b, s, MLA_HEADS * MLA_V), BF16),
        compiler_params=_params(("parallel", "parallel", "parallel")),
        name="mla_attention",
    )(q, k, v)


def _na_kernel(q_ref, k_ref, v_ref, bias_ref, o_ref, s0, s1, p0, p1, l0, l1, *, ctx, rows):
    w = GRID_W
    win = NA_KH * w
    s_scr, p_scr, l_scr = (s0, s1), (p0, p1), (l0, l1)
    lane = lax.broadcasted_iota(jnp.int32, (w, 2 * NA_DH), 1)
    first = lane < NA_DH
    kc = k_ref[0, 0:ctx, :]
    vc = v_ref[0, 0:ctx, :]

    def split_heads(qr, first_mask):
        zero = jnp.zeros_like(qr)
        return jnp.concatenate([jnp.where(first_mask, qr, zero), jnp.where(first_mask, zero, qr)], axis=0)

    def merge_heads(o2):
        n = o2.shape[0] // 2
        lane_n = lax.broadcasted_iota(jnp.int32, (n, 2 * NA_DH), 1)
        return jnp.where(lane_n < NA_DH, o2[:n], o2[n:])

    lane_c = lax.broadcasted_iota(jnp.int32, (ctx, 2 * NA_DH), 1)
    qc2 = split_heads(q_ref[0, 0:ctx, :], lane_c < NA_DH)
    o_ref[0, 0:ctx, :] = merge_heads(_softmax_pv(_dot_nt(qc2, kc), vc)).astype(o_ref.dtype)

    def window_start(r):
        return pl.multiple_of(ctx + jnp.clip(r - NA_KH // 2, 0, rows - NA_KH) * w, w)

    def scores(r, slot):
        pat = jnp.where(r < NA_KH // 2, r,
                        jnp.where(r <= rows - NA_KH // 2, NA_KH // 2, r - (rows - NA_KH)))
        q2 = split_heads(q_ref[0, pl.ds(pl.multiple_of(ctx + r * w, w), w), :], first)
        kw = k_ref[0, pl.ds(window_start(r), win), :]
        s_scr[slot][:, 0:win] = _dot_nt(q2, kw) + bias_ref[0, pat]
        s_scr[slot][:, win:win + ctx] = _dot_nt(q2, kc)

    def softmax(slot):
        s = s_scr[slot][...]
        m = jnp.max(s, axis=-1, keepdims=True)
        p = jnp.exp(s - m)
        l_scr[slot][...] = jnp.broadcast_to(jnp.sum(p, axis=-1, keepdims=True), (2 * w, 2 * NA_DH))
        p_scr[slot][...] = p.astype(BF16)

    def values(r, slot):
        vw = v_ref[0, pl.ds(window_start(r), win), :]
        o2 = (_dot(p_scr[slot][:, 0:win], vw) + _dot(p_scr[slot][:, win:win + ctx], vc)) / l_scr[slot][...]
        o_ref[0, pl.ds(pl.multiple_of(ctx + r * w, w), w), :] = merge_heads(o2).astype(o_ref.dtype)

    scores(0, 0)
    scores(1, 1)
    softmax(0)

    def two_rows(u, carry):
        t = 2 * u
        scores(t, 0)
        softmax(1)
        values(t - 2, 0)
        scores(t + 1, 1)
        softmax(0)
        values(t - 1, 1)
        return carry

    lax.fori_loop(1, rows // 2, two_rows, 0)
    softmax(1)
    values(rows - 2, 0)
    values(rows - 1, 1)


def _na_attention(q, k, v, bias, ctx):
    b, s, width = q.shape
    rows = (s - ctx) // GRID_W
    assert rows % 2 == 0
    pairs = width // (2 * NA_DH)
    keys = NA_KH * GRID_W + ctx
    blk = pl.BlockSpec((1, s, 2 * NA_DH), lambda i, h: (i, 0, h))
    scratch = ([pltpu.VMEM((2 * GRID_W, keys), F32)] * 2 + [pltpu.VMEM((2 * GRID_W, keys), BF16)] * 2
               + [pltpu.VMEM((2 * GRID_W, 2 * NA_DH), F32)] * 2)
    return pl.pallas_call(
        functools.partial(_na_kernel, ctx=ctx, rows=rows),
        grid=(b, pairs),
        in_specs=[blk, blk, blk,
                  pl.BlockSpec((1, NA_PATTERNS, 2 * GRID_W, NA_KH * GRID_W), lambda i, h: (h, 0, 0, 0))],
        out_specs=blk,
        out_shape=jax.ShapeDtypeStruct((b, s, width), BF16),
        scratch_shapes=scratch,
        compiler_params=_params(("parallel", "parallel")),
        name="na_attention",
    )(q, k, v, bias)


def _na_bias_kernel(rpb_ref, o_ref, *, rows):
    w, kh, kw = GRID_W, NA_KH, NA_KW
    pair = pl.program_id(0)
    qi = lax.broadcasted_iota(jnp.int32, (w, w), 0)
    ki = lax.broadcasted_iota(jnp.int32, (w, w), 1)
    col_idx = jnp.clip(ki - qi, -(kw - 1), kw - 1) + kw - 1
    c_start = jnp.clip(qi - kw // 2, 0, w - kw)
    masked = jnp.where((ki >= c_start) & (ki < c_start + kw), 0.0, NEG_BIG)
    rep = (0, 1, 2, 3, 4, rows - 3, rows - 2, rows - 1)
    for hh in range(2):
        head = 2 * pair + hh
        tiles = []
        for dr in range(2 * kh - 1):
            t = masked
            for dc in range(2 * kw - 1):
                t = jnp.where(col_idx == dc, masked + rpb_ref[head, dr, dc], t)
            tiles.append(t)
        for p, r in enumerate(rep):
            r_start = min(max(r - kh // 2, 0), rows - kh)
            for j in range(kh):
                o_ref[0, p, hh * w:(hh + 1) * w, j * w:(j + 1) * w] = tiles[r_start + j - r + kh - 1]


def _na_bias_table(rpb, rows):
    h = rpb.shape[0]
    shape = (h // 2, NA_PATTERNS, 2 * GRID_W, NA_KH * GRID_W)
    return pl.pallas_call(
        functools.partial(_na_bias_kernel, rows=rows),
        grid=(h // 2,),
        in_specs=[pl.BlockSpec(memory_space=pltpu.SMEM)],
        out_specs=pl.BlockSpec((1,) + shape[1:], lambda i: (i, 0, 0, 0)),
        out_shape=jax.ShapeDtypeStruct(shape, F32),
        compiler_params=_params(("parallel",)),
        name="na_bias_table",
    )(rpb.astype(F32))


def _gdn_kernel(q_ref, k_ref, v_ref, z_ref, ba_ref, cwq_ref, cwk_ref, cwv_ref, alog_ref, dtb_ref, gain_ref,
                bd_ref, tril_ref, triu_ref, o_ref, qs, ks, vs, wq_s, ut_s, qkk_s, gl_s, o_s, *, ctx, seq, hp):
    c = GDN_CHUNK
    dk = GDN_DK
    grp = GDN_GROUP
    gr = grp * c
    nc = seq // c
    ncc = ctx // c
    ng = seq // gr

    row = lax.broadcasted_iota(jnp.int32, (seq, 128), 0)
    pad = SHORT_CONV // 2
    tap_ok = {d: (row + d >= 0) & (row + d < seq) & ((row < ctx) == (row + d < ctx))
              for d in range(-pad, pad + 1) if d != 0}

    def short_conv(x, cw):
        acc = x * cw[pad:pad + 1, :]
        for j in range(SHORT_CONV):
            d = j - pad
            if d != 0:
                acc = acc + jnp.where(tap_ok[d], pltpu.roll(x, (-d) % seq, axis=0), 0.0) * cw[j:j + 1, :]
        return _silu(acc)

    def l2n(x):
        return x * lax.rsqrt(jnp.sum(x * x, axis=-1, keepdims=True) + EPS)

    w8 = 2 * gr
    sup = max(m for m in (1, 2, 3) if ng % m == 0)
    wide = sup * w8
    ii = lax.broadcasted_iota(jnp.int32, (c, wide), 0)
    lane8 = lax.broadcasted_iota(jnp.int32, (c, wide), 1)
    jj = lane8 & (c - 1)
    fwd8 = ((lane8 >> 6) & 1) == 0
    lchunk = lax.broadcasted_iota(jnp.int32, (c, w8), 1) >> 7
    incl = (fwd8 & (ii >= jj)) | (jnp.logical_not(fwd8) & (ii <= jj))
    strict = incl & (ii != jj)
    eye8 = (ii == jj).astype(F32)
    same8 = (ii >> 3) == (jj >> 3)
    lane_g = lax.broadcasted_iota(jnp.int32, (gr, 128), 1)
    fwd_half = lane_g < c
    fwd_half_c = lax.broadcasted_iota(jnp.int32, (c, 128), 1) < c

    def bd(y):
        yb = y.astype(BF16)
        return [jnp.concatenate([yb[:, h * gr:(h + 1) * gr]] * grp, axis=0) * bd_ref[...]
                for h in range(wide // gr)]

    def mm(x, ybd):
        xb = x.astype(BF16)
        return jnp.concatenate([_dot(xb[:, h * gr:(h + 1) * gr], ybd[h]) for h in range(wide // gr)], axis=1)

    def diag8(f):
        out = f[0:c]
        for b in range(1, grp):
            out = jnp.where(lchunk == b, f[b * c:(b + 1) * c], out)
        return out

    def inverse8(a):
        p = jnp.where(same8, -a, 0.0)
        t = eye8 + p
        p2 = mm(p, bd(p))
        r = mm(jnp.concatenate([t, p2], axis=0), bd(p2))
        t = t + r[0:c]
        t = t + mm(t, bd(r[c:2 * c]))
        for sh in (3, 4, 5):
            off = ((ii >> (sh + 1)) == (jj >> (sh + 1))) & ((ii >> sh) != (jj >> sh))
            l_mat = jnp.where(off, a, 0.0)
            t = t - mm(t, bd(mm(l_mat, bd(t))))
        return t

    def chunk_sums(x):
        hi = x.astype(BF16)
        r1 = x - hi.astype(F32)
        mid = r1.astype(BF16)
        lo = (r1 - mid.astype(F32)).astype(BF16)
        parts = jnp.concatenate([hi, mid, lo], axis=1)
        pre = _dot(tril_ref[...], parts)
        suf = _dot(triu_ref[...], parts)
        fold = lambda r: r[:, 0:128] + r[:, 128:256] + r[:, 256:384]
        return fold(pre), fold(suf)

    def pick(x, lane_id):
        col = jnp.sum(jnp.where(lane_g == lane_id, x, 0.0), axis=-1, keepdims=True)
        return jnp.broadcast_to(col, (gr, 128))

    def chunks(x):
        return [x[b * c:(b + 1) * c] for b in range(grp)]

    def across(xf, xb):
        both = jnp.where(fwd_half, xf, xb)
        return jnp.concatenate(chunks(both), axis=1)

    def interleave_rows(xf, xb):
        return jnp.concatenate([s for pair in zip(chunks(xf), chunks(xb)) for s in pair], axis=0)

    def group_inputs(gi, head):
        r0 = pl.multiple_of(gi * gr, gr)
        q = qs[pl.ds(r0, gr), :]
        k = ks[pl.ds(r0, gr), :]
        v = vs[pl.ds(r0, gr), :]
        ba = ba_ref[0, pl.ds(r0, gr), :]
        beta_all = jax.nn.sigmoid(ba)
        g_all = -jnp.exp(alog_ref[...]) * _softplus(ba + dtb_ref[...])
        pre_all, suf_all = chunk_sums(g_all)
        lane_f = 2 * GDN_HEADS + head
        lane_b = 3 * GDN_HEADS + head
        beta = (pick(beta_all, head), pick(beta_all, GDN_HEADS + head))
        gc = (pick(pre_all, lane_f), pick(suf_all, lane_b))
        grest = (pick(suf_all - g_all, lane_f), pick(pre_all - g_all, lane_b))
        kb = k.astype(BF16)
        kdup = interleave_rows(kb, kb)
        r = _dot_nt(jnp.concatenate([q.astype(BF16), kb], axis=0), kdup)
        g_j = interleave_rows(gc[0], gc[1]).T[0:c]
        return dict(q=q, k=k, v=v, beta=beta, gc=gc, grest=grest, qk8=diag8(r[0:gr]), kk8=diag8(r[gr:2 * gr]),
                    diff=across(gc[0], gc[1]) - g_j, beta8=across(beta[0], beta[1]))

    def prep(si, carry, head, hh):
        groups = [group_inputs(si * sup + g, head) for g in range(sup)]
        cat = lambda name: jnp.concatenate([g[name] for g in groups], axis=1)
        decay = jnp.where(incl, jnp.exp(jnp.minimum(cat("diff"), 0.0)), 0.0)
        t_all = inverse8(jnp.where(strict, cat("beta8") * cat("kk8") * decay, 0.0))
        qkd_all = cat("qk8") * decay
        for gidx, grp_in in enumerate(groups):
            q, k, v, beta, gc, grest = (grp_in[name] for name in ("q", "k", "v", "beta", "gc", "grest"))
            t8 = t_all[:, gidx * w8:(gidx + 1) * w8]
            qkd8 = qkd_all[:, gidx * w8:(gidx + 1) * w8]
            rhs, qd, kd, eg = [], [], [], []
            for d in range(2):
                e = jnp.exp(gc[d])
                eg.append(e)
                rhs.append(jnp.concatenate([(k * (beta[d] * e)).astype(BF16), (v * beta[d]).astype(BF16)], axis=1))
                qd.append((q * e).astype(BF16))
                kd.append(k * jnp.exp(grest[d]))
            for b in range(grp):
                n = (si * sup + gidx) * grp + b
                rows = slice(b * c, (b + 1) * c)
                tb = t8[:, 2 * c * b:2 * c * (b + 1)]
                lhs = jnp.concatenate([jnp.where(fwd_half_c, tb, 0.0), jnp.where(fwd_half_c, 0.0, tb)], axis=0)
                sol = _dot(lhs.astype(BF16), jnp.concatenate([rhs[0][rows], rhs[1][rows]], axis=0))
                for d in range(2):
                    sd = sol[d * c:(d + 1) * c]
                    wq_s[2 * hh + d, n] = jnp.concatenate([sd[:, 0:dk].astype(BF16), qd[d][rows]], axis=0)
                    ut_s[2 * hh + d, n] = sd[:, dk:]
                    lo = 2 * c * b + c * d
                    qkk_s[2 * hh + d, n] = jnp.concatenate([qkd8[:, lo:lo + c], kd[d][rows].T], axis=0).astype(BF16)
                    edge = b * c + c - 1 if d == 0 else b * c
                    gl_s[2 * hh + d, n] = jnp.broadcast_to(eg[d][edge:edge + 1, :], (8, 128))
        return carry

    for hh in range(hp):
        head = pl.program_id(1) * hp + hh
        ls = slice(hh * 128, (hh + 1) * 128)
        qs[...] = l2n(short_conv(q_ref[0, :, ls].astype(F32), cwq_ref[:, ls])) * (dk ** -0.5)
        ks[...] = l2n(short_conv(k_ref[0, :, ls].astype(F32), cwk_ref[:, ls]))
        vs[...] = short_conv(v_ref[0, :, ls].astype(F32), cwv_ref[:, ls])
        lax.fori_loop(0, ng // sup, functools.partial(prep, head=head, hh=hh), 0)

    def step(i, states):
        n_bwd = jnp.where(i < ncc, ncc - 1 - i, nc - 1 - (i - ncc))
        new = []
        for ch in range(2 * hp):
            n = i if ch % 2 == 0 else n_bwd
            state = states[ch]
            sb = state.astype(BF16)
            r1 = _dot(wq_s[ch, n], sb)
            ub = (ut_s[ch, n] - r1[0:c]).astype(BF16)
            r2 = _dot(qkk_s[ch, n], ub)
            o_s[ch, pl.ds(pl.multiple_of(n * c, c), c), :] = r1[c:2 * c] + r2[0:c]
            new.append(gl_s[ch, n][0:1, :] * state + r2[c:c + dk])
        return tuple(new)

    zero = jnp.zeros((dk, GDN_DV), F32)
    lax.fori_loop(0, nc, step, (zero,) * (2 * hp))

    for hh in range(hp):
        ls = slice(hh * 128, (hh + 1) * 128)
        o = _rms(o_s[2 * hh] + o_s[2 * hh + 1], gain_ref[...]) * _silu(z_ref[0, :, ls].astype(F32))
        o_ref[0, :, ls] = o.astype(o_ref.dtype)


def _gdn(qkv, z, ba, conv_w, alog_row, dtb_row, gain, ctx):
    b, s, _ = qkv.shape
    h = GDN_HEADS
    c = GDN_CHUNK
    assert s % (GDN_GROUP * c) == 0
    nc = s // c
    gr = GDN_GROUP * c
    blk = lambda n: jnp.arange(n) // c
    bd_ones = (blk(gr)[:, None] == blk(gr)[None, :]).astype(BF16)
    same = blk(gr)[:, None] == blk(gr)[None, :]
    tri_lo = (same & (jnp.arange(gr)[:, None] >= jnp.arange(gr)[None, :])).astype(BF16)
    tri_up = (same & (jnp.arange(gr)[:, None] <= jnp.arange(gr)[None, :])).astype(BF16)
    hp = GDN_HEADS_PER_STEP
    lanes = hp * 128
    col = lambda off: pl.BlockSpec((1, s, lanes), lambda i, j: (i, 0, off // hp + j))
    cw = lambda off: pl.BlockSpec((SHORT_CONV, lanes), lambda i, j: (0, off // hp + j))
    ch = 2 * hp
    scratch = [pltpu.VMEM((s, 128), F32), pltpu.VMEM((s, 128), F32), pltpu.VMEM((s, 128), F32),
               pltpu.VMEM((ch, nc, 2 * c, 128), BF16), pltpu.VMEM((ch, nc, c, 128), F32),
               pltpu.VMEM((ch, nc, c + GDN_DK, c), BF16), pltpu.VMEM((ch, nc, 8, 128), F32),
               pltpu.VMEM((ch, s, 128), F32)]
    return pl.pallas_call(
        functools.partial(_gdn_kernel, ctx=ctx, seq=s, hp=hp),
        grid=(b, h // hp),
        in_specs=[col(0), col(h), col(2 * h),
                  pl.BlockSpec((1, s, lanes), lambda i, j: (i, 0, j)),
                  pl.BlockSpec((1, s, 128), lambda i, j: (i, 0, 0)),
                  cw(0), cw(h), cw(2 * h),
                  _const_spec((1, 128)), _const_spec((1, 128)), _const_spec((1, 128)),
                  _const_spec(bd_ones.shape), _const_spec(tri_lo.shape), _const_spec(tri_up.shape)],
        out_specs=pl.BlockSpec((1, s, lanes), lambda i, j: (i, 0, j)),
        out_shape=jax.ShapeDtypeStruct((b, s, h * GDN_DV), BF16),
        scratch_shapes=scratch,
        compiler_params=_params(("parallel", "parallel")),
        name="gdn",
    )(qkv, qkv, qkv, z, ba, conv_w, conv_w, conv_w, alog_row, dtb_row, gain.reshape(1, -1), bd_ones, tri_lo, tri_up)


def _rope_tables(n_lat, ctx):
    t = jnp.arange(n_lat)
    row = (t // GRID_W).astype(F32)
    col = (t % GRID_W).astype(F32)
    quarter = MLA_ROPE // 4
    inv_freq = ROPE_BASE ** (-jnp.arange(quarter, dtype=F32) / quarter)
    ang_r = row[:, None] * inv_freq
    ang_c = col[:, None] * inv_freq
    ang = jnp.concatenate([ang_r, ang_r, ang_c, ang_c], axis=-1)
    cos = jnp.concatenate([jnp.ones((ctx, MLA_ROPE), F32), jnp.cos(ang)], axis=0)
    sin = jnp.concatenate([jnp.zeros((ctx, MLA_ROPE), F32), jnp.sin(ang)], axis=0)
    s = ctx + n_lat
    one = jnp.ones((s, MLA_NOPE), F32)
    zn = jnp.zeros((s, MLA_NOPE), F32)
    zp = jnp.zeros((s, MLA_HEAD_PAD - MLA_NOPE - MLA_ROPE), F32)
    cosq = jnp.concatenate([one, cos, zp], axis=1)
    sinq = jnp.concatenate([zn, sin, zp], axis=1)
    zk = jnp.zeros((s, 128 - MLA_ROPE), F32)
    cosk = jnp.concatenate([cos, zk], axis=1)
    sink = jnp.concatenate([sin, zk], axis=1)
    return cosq, sinq, cosk, sink


def _rot_cols(w):
    q = MLA_ROPE // 4
    a, b, c, d = w[..., 0:q], w[..., q:2 * q], w[..., 2 * q:3 * q], w[..., 3 * q:4 * q]
    return jnp.concatenate([-b, a, -d, c], axis=-1)


def _even_weights(w_in, w_uq, w_ukv, w_out):
    d = w_in.shape[0]
    h = GDN_HEADS
    qkv_w = 2 * h * GDN_DK + h * GDN_DV
    vw = h * GDN_DV
    o = 0
    w_qkv = w_in[:, o:o + qkv_w]; o += qkv_w
    w_z = w_in[:, o:o + vw]; o += vw
    w_ba = w_in[:, o:o + 4 * h]; o += 4 * h
    w_cq = w_in[:, o:o + MLA_Q_RANK]; o += MLA_Q_RANK
    w_ckv = w_in[:, o:o + MLA_KV_RANK]; o += MLA_KV_RANK
    w_kr = w_in[:, o:o + MLA_ROPE]
    z64 = jnp.zeros((d, 128 - MLA_ROPE), w_in.dtype)
    w_ba = jnp.concatenate([w_ba, jnp.zeros((d, 128 - 4 * h), w_in.dtype)], axis=1)
    w_mla = jnp.concatenate([w_cq, w_ckv, w_kr, z64, _rot_cols(w_kr), z64], axis=1)
    proj = [w_qkv.astype(BF16), w_z.astype(BF16), w_ba.astype(BF16), w_mla.astype(BF16)]

    rq = MLA_Q_RANK
    wq = w_uq.reshape(rq, MLA_HEADS, MLA_NOPE + MLA_ROPE)
    zpad = jnp.zeros((rq, MLA_HEADS, MLA_HEAD_PAD - MLA_NOPE - MLA_ROPE), w_uq.dtype)
    znope = jnp.zeros((rq, MLA_HEADS, MLA_NOPE), w_uq.dtype)
    wqa = jnp.concatenate([wq, zpad], axis=-1).reshape(rq, -1)
    wqb = jnp.concatenate([znope, _rot_cols(wq[..., MLA_NOPE:]), zpad], axis=-1).reshape(rq, -1)
    rk = MLA_KV_RANK
    wkv = w_ukv.reshape(rk, MLA_HEADS, MLA_NOPE + MLA_V)
    wk = jnp.concatenate([wkv[..., :MLA_NOPE], jnp.zeros((rk, MLA_HEADS, MLA_HEAD_PAD - MLA_NOPE), w_ukv.dtype)],
                         axis=-1).reshape(rk, -1)
    wv = wkv[..., MLA_NOPE:].reshape(rk, -1)
    eye = jnp.eye(MLA_ROPE, dtype=F32)
    blk = jnp.concatenate([jnp.zeros((MLA_ROPE, MLA_NOPE), F32), eye,
                           jnp.zeros((MLA_ROPE, MLA_HEAD_PAD - MLA_NOPE - MLA_ROPE), F32)], axis=1)
    pk = jnp.concatenate([jnp.tile(blk, (1, MLA_HEADS)),
                          jnp.zeros((128 - MLA_ROPE, MLA_HEADS * MLA_HEAD_PAD), F32)], axis=0)
    up = [wqa.astype(BF16), wqb.astype(BF16), wk.astype(BF16), wv.astype(BF16), pk.astype(BF16)]
    out = [w_out[:vw].astype(BF16), w_out[vw:].astype(BF16)]
    return proj, up, out


def _gate_rows(a_log, dt_bias):
    n = 2 * GDN_HEADS
    z = jnp.zeros((n,), F32)
    tail = jnp.zeros((128 - 2 * n,), F32)
    alog_row = jnp.concatenate([z, a_log.astype(F32).reshape(n), tail]).reshape(1, 128)
    dtb_row = jnp.concatenate([z, dt_bias.astype(F32).reshape(n), tail]).reshape(1, 128)
    return alog_row, dtb_row


def kernel(x, c, ctx, c_ctx, ada_w, ada_b, norm_mix_g, norm_ffn_g, ev_w_in, ev_conv_qkv, ev_a_log, ev_dt_bias,
           ev_gdn_norm, ev_q_norm, ev_kv_norm, ev_w_uq, ev_w_ukv, ev_w_out, od_w_qkv, od_rpb, od_w_out,
           ffn_w_in, ffn_conv, ffn_conv_b, ffn_w_out, final_g):
    b, n_lat, d = x.shape
    n_ctx = ctx.shape[1]
    depth = ada_w.shape[0]
    tm = n_ctx
    assert n_lat % tm == 0 and n_lat % GRID_W == 0 and tm % GDN_CHUNK == 0
    rows = n_lat // GRID_W
    assert rows >= NA_KH

    r_pad = -(b + 1) % 8
    cond = jnp.concatenate([c, c_ctx[None, :], jnp.zeros((r_pad, d), c.dtype)], axis=0)
    mods = _ada_table(cond, ada_w, ada_b).reshape(depth, b + 1 + r_pad, 6, d)
    tabs = _rope_tables(n_lat, n_ctx)
    d_ff = ffn_w_out.shape[1]

    xa = jnp.concatenate([ctx, x], axis=1)
    for i in range(depth):
        j = i // 2
        m = mods[i]
        if i % 2 == 0:
            proj_w, up_w, out_w = _even_weights(ev_w_in[j], ev_w_uq[j], ev_w_ukv[j], ev_w_out[j])
            qkv, z, ba, p_mla = _norm_mod_matmul(xa, m, norm_mix_g[i], proj_w, [BF16, BF16, F32, F32], 0, tm)
            alog_row, dtb_row = _gate_rows(ev_a_log[j], ev_dt_bias[j])
            y_a = _gdn(qkv, z, ba, ev_conv_qkv[j], alog_row, dtb_row, ev_gdn_norm[j], n_ctx)
            q, k, v = _mla_up(p_mla, tabs, ev_q_norm[j], ev_kv_norm[j], *up_w, tm)
            y_b = _mla_attention(q, k, v, n_ctx, tm)
            xa = _outproj_residual(xa, m, [y_a, y_b], out_w, 2, tm)
        else:
            w = od_w_qkv[j]
            nw = w.shape[1] // 3
            ws = [(w[:, :nw] * (NA_DH ** -0.5)).astype(BF16), w[:, nw:2 * nw].astype(BF16),
                  w[:, 2 * nw:].astype(BF16)]
            q, k, v = _norm_mod_matmul(xa, m, norm_mix_g[i], ws, [BF16, BF16, BF16], 0, tm)
            y = _na_attention(q, k, v, _na_bias_table(od_rpb[j], rows), n_ctx)
            xa = _outproj_residual(xa, m, [y], [od_w_out[j].astype(BF16)], 2, tm)
        wi = ffn_w_in[i]
        xa = _conv_ffn_residual(xa, m, norm_ffn_g[i], wi[:, :d_ff].astype(BF16), wi[:, d_ff:].astype(BF16),
                                ffn_conv[i], ffn_conv_b[i], ffn_w_out[i].astype(BF16), tm,
                                final_g=final_g if i == depth - 1 else None)
    return xa
```

```python
import functools
import math

import jax
import jax.numpy as jnp
from jax import lax
from jax.experimental import pallas as pl
from jax.experimental.pallas import tpu as pltpu

EPS = 1e-6
ROPE_BASE = 10000.0
GRID_W = 64

GDN_HEADS = 4
GDN_DK = 128
GDN_DV = 128
GDN_CHUNK = 64
GDN_GROUP = 4
GDN_HEADS_PER_STEP = 2
SHORT_CONV = 5

MLA_HEADS = 4
MLA_NOPE = 128
MLA_ROPE = 64
MLA_V = 128
MLA_Q_RANK = 256
MLA_KV_RANK = 256
MLA_HEAD_PAD = 256

NA_HEADS = 16
NA_DH = 64
NA_KH = 8
NA_KW = 16
NA_PATTERNS = 8

FFN_CONV = 3
HALO = 16

VMEM_LIMIT = 56 * 1024 * 1024
NEG_BIG = -1e30

BF16 = jnp.bfloat16
F32 = jnp.float32


def _dot(a, b):
    return jnp.dot(a, b, preferred_element_type=F32)


def _dot_nt(a, b):
    return lax.dot_general(a, b, (((1,), (1,)), ((), ())), preferred_element_type=F32)


def _silu(x):
    return x * jax.nn.sigmoid(x)


def _softplus(x):
    return jnp.maximum(x, 0.0) + jnp.log1p(jnp.exp(-jnp.abs(x)))


def _rms(x, g):
    return x * lax.rsqrt(jnp.mean(x * x, axis=-1, keepdims=True) + EPS) * g


def _params(sem):
    return pltpu.CompilerParams(dimension_semantics=sem, vmem_limit_bytes=VMEM_LIMIT)


def _const_spec(shape):
    nd = len(shape)
    return pl.BlockSpec(shape, lambda *_: (0,) * nd)


def _ada_kernel(c_ref, w_ref, b_ref, o_ref):
    s = _silu(c_ref[...]).astype(BF16)
    o_ref[0] = _dot(s, w_ref[0]) + b_ref[0]


def _ada_table(cond, ada_w, ada_b):
    depth, d, n = ada_w.shape
    r = cond.shape[0]
    tn = 1536
    return pl.pallas_call(
        _ada_kernel,
        grid=(depth, n // tn),
        in_specs=[pl.BlockSpec((r, d), lambda i, j: (0, 0)),
                  pl.BlockSpec((1, d, tn), lambda i, j: (i, 0, j)),
                  pl.BlockSpec((1, 1, tn), lambda i, j: (i, 0, j))],
        out_specs=pl.BlockSpec((1, r, tn), lambda i, j: (i, 0, j)),
        out_shape=jax.ShapeDtypeStruct((depth, r, n), F32),
        compiler_params=_params(("parallel", "parallel")),
        name="ada_table",
    )(cond, ada_w.astype(BF16), ada_b.reshape(depth, 1, n))


def _nmm_kernel(x_ref, mod_ref, g_ref, *rest, n_out, shift_idx):
    w_refs, o_refs = rest[:n_out], rest[n_out:]
    shift = mod_ref[0, shift_idx:shift_idx + 1, :]
    scale = mod_ref[0, shift_idx + 1:shift_idx + 2, :]
    h = _rms(x_ref[0], g_ref[...]) * (1.0 + scale) + shift
    hb = h.astype(BF16)
    for w_ref, o_ref in zip(w_refs, o_refs):
        o_ref[0] = _dot(hb, w_ref[...]).astype(o_ref.dtype)


def _norm_mod_matmul(xa, mods, g, ws, out_dtypes, shift_idx, tm):
    b, s, d = xa.shape
    n_out = len(ws)
    in_specs = [pl.BlockSpec((1, tm, d), lambda i, t: (i, t, 0)),
                pl.BlockSpec((1, 6, d), lambda i, t: (jnp.where(t == 0, b, i), 0, 0)),
                _const_spec((1, d))]
    in_specs += [_const_spec(w.shape) for w in ws]
    out_specs = [pl.BlockSpec((1, tm, w.shape[1]), lambda i, t: (i, t, 0)) for w in ws]
    out_shape = [jax.ShapeDtypeStruct((b, s, w.shape[1]), dt) for w, dt in zip(ws, out_dtypes)]
    return pl.pallas_call(
        functools.partial(_nmm_kernel, n_out=n_out, shift_idx=shift_idx),
        grid=(b, s // tm),
        in_specs=in_specs, out_specs=out_specs, out_shape=out_shape,
        compiler_params=_params(("parallel", "parallel")),
        name="norm_mod_matmul",
    )(xa, mods, g.reshape(1, d), *ws)


def _mix_ffn_kernel(x_ref, xp_ref, xn_ref, *rest, n_in, tm, n_tiles, t_off, final):
    y_refs = rest[:3 * n_in]
    w_refs = rest[3 * n_in:4 * n_in]
    mod_ref, g_ref, wu_ref, wg_ref, cw_ref, cb_ref, wo_ref, fg_ref, o_ref = rest[4 * n_in:]
    t = pl.program_id(1) + t_off
    ext = lambda main, prev, nxt: jnp.concatenate([prev[0], main[0], nxt[0]], axis=0)
    proj = _dot(ext(*y_refs[0:3]), w_refs[0][...])
    for j in range(1, n_in):
        proj = proj + _dot(ext(*y_refs[3 * j:3 * j + 3]), w_refs[j][...])
    x_ext = ext(x_ref, xp_ref, xn_ref) + mod_ref[0, 2:3, :] * proj
    x = x_ext[HALO:HALO + tm]
    shift = mod_ref[0, 3:4, :]
    scale = mod_ref[0, 4:5, :]
    gate = mod_ref[0, 5:6, :]
    h_ext = (_rms(x_ext, g_ref[...]) * (1.0 + scale) + shift).astype(BF16)
    gt = _dot(h_ext, wg_ref[...])
    u = _dot(h_ext[HALO:HALO + tm], wu_ref[...])
    prev_ok = (t >= 2).astype(F32)
    next_ok = jnp.logical_and(t >= 1, t < n_tiles - 1).astype(F32)
    row = lax.broadcasted_iota(jnp.int32, (tm + 2 * HALO, 1), 0)
    keep = jnp.where(row < HALO, prev_ok, jnp.where(row >= tm + HALO, next_ok, 1.0))
    gt = gt * keep
    g_prev = pltpu.roll(gt, 1, axis=0)[HALO:HALO + tm]
    g_next = pltpu.roll(gt, tm + 2 * HALO - 1, axis=0)[HALO:HALO + tm]
    g_mid = gt[HALO:HALO + tm]
    conv = g_prev * cw_ref[0:1, :] + g_mid * cw_ref[1:2, :] + g_next * cw_ref[2:3, :] + cb_ref[...]
    act = (_silu(conv) * u).astype(BF16)
    y = x + gate * _dot(act, wo_ref[...])
    o_ref[0] = _rms(y, fg_ref[...]) if final else y


def _mix_ffn_residual(xa, mods, ys, w_outs, g, wu, wg, conv_w, conv_b, wo, tm, final_g=None):
    b, s, d = xa.shape
    f = wu.shape[1]
    n_tiles = s // tm
    hb = tm // HALO
    last = s // HALO - 1
    final = final_g is not None
    t_off = 1 if final else 0
    fg = final_g if final else g
    mod_row = (lambda i, t: (i, 0, 0)) if final else (lambda i, t: (jnp.where(t == 0, b, i), 0, 0))
    main = lambda w: pl.BlockSpec((1, tm, w), lambda i, t: (i, t + t_off, 0))
    prev = lambda w: pl.BlockSpec((1, HALO, w), lambda i, t: (i, jnp.maximum((t + t_off) * hb - 1, 0), 0))
    nxt = lambda w: pl.BlockSpec((1, HALO, w), lambda i, t: (i, jnp.minimum((t + t_off + 1) * hb, last), 0))
    in_specs = [main(d), prev(d), nxt(d)]
    args = [xa, xa, xa]
    for y in ys:
        in_specs += [main(y.shape[2]), prev(y.shape[2]), nxt(y.shape[2])]
        args += [y, y, y]
    in_specs += [_const_spec(w.shape) for w in w_outs]
    in_specs += [pl.BlockSpec((1, 6, d), mod_row), _const_spec((1, d)), _const_spec((d, f)), _const_spec((d, f)),
                 _const_spec((FFN_CONV, f)), _const_spec((1, f)), _const_spec((f, d)), _const_spec((1, d))]
    args += list(w_outs) + [mods, g.reshape(1, d), wu, wg, conv_w, conv_b.reshape(1, f), wo, fg.reshape(1, d)]
    return pl.pallas_call(
        functools.partial(_mix_ffn_kernel, n_in=len(ys), tm=tm, n_tiles=n_tiles, t_off=t_off, final=final),
        grid=(b, n_tiles - t_off),
        in_specs=in_specs,
        out_specs=pl.BlockSpec((1, tm, d), lambda i, t: (i, t, 0)),
        out_shape=jax.ShapeDtypeStruct((b, s - t_off * tm, d), F32),
        compiler_params=_params(("parallel", "parallel")),
        name="mix_ffn",
    )(*args)


def _mla_up_kernel(p_ref, cq_ref, sq_ref, ck_ref, sk_ref, qn_ref, kvn_ref,
                   wqa_ref, wqb_ref, wk_ref, wv_ref, pk_ref, q_ref, k_ref, v_ref, *, scale):
    r = MLA_Q_RANK
    cq = _rms(p_ref[0, :, 0:r], qn_ref[...]).astype(BF16)
    ckv = _rms(p_ref[0, :, r:2 * r], kvn_ref[...]).astype(BF16)
    kr = p_ref[0, :, 2 * r:2 * r + 128]
    krr = p_ref[0, :, 2 * r + 128:2 * r + 256]
    k_rope = (kr * ck_ref[...] + krr * sk_ref[...]).astype(BF16)
    k_ref[0] = (_dot(ckv, wk_ref[...]) + _dot(k_rope, pk_ref[...])).astype(BF16)
    v_ref[0] = _dot(ckv, wv_ref[...]).astype(BF16)
    qa = _dot(cq, wqa_ref[...])
    qb = _dot(cq, wqb_ref[...])
    cq_t = cq_ref[...] * scale
    sq_t = sq_ref[...] * scale
    for h in range(MLA_HEADS):
        sl = slice(h * MLA_HEAD_PAD, (h + 1) * MLA_HEAD_PAD)
        q_ref[0, :, sl] = (qa[:, sl] * cq_t + qb[:, sl] * sq_t).astype(BF16)


def _mla_up(p, tabs, q_norm, kv_norm, wqa, wqb, wk, wv, pk, tm):
    b, s, pw = p.shape
    cosq, sinq, cosk, sink = tabs
    hw = MLA_HEADS * MLA_HEAD_PAD
    vw = MLA_HEADS * MLA_V
    scale = (MLA_NOPE + MLA_ROPE) ** -0.5
    row = lambda w: pl.BlockSpec((tm, w), lambda i, t: (t, 0))
    return pl.pallas_call(
        functools.partial(_mla_up_kernel, scale=scale),
        grid=(b, s // tm),
        in_specs=[pl.BlockSpec((1, tm, pw), lambda i, t: (i, t, 0)),
                  row(MLA_HEAD_PAD), row(MLA_HEAD_PAD), row(128), row(128),
                  _const_spec((1, MLA_Q_RANK)), _const_spec((1, MLA_KV_RANK)),
                  _const_spec(wqa.shape), _const_spec(wqb.shape), _const_spec(wk.shape),
                  _const_spec(wv.shape), _const_spec(pk.shape)],
        out_specs=[pl.BlockSpec((1, tm, hw), lambda i, t: (i, t, 0)),
                   pl.BlockSpec((1, tm, hw), lambda i, t: (i, t, 0)),
                   pl.BlockSpec((1, tm, vw), lambda i, t: (i, t, 0))],
        out_shape=[jax.ShapeDtypeStruct((b, s, hw), BF16),
                   jax.ShapeDtypeStruct((b, s, hw), BF16),
                   jax.ShapeDtypeStruct((b, s, vw), BF16)],
        compiler_params=_params(("parallel", "parallel")),
        name="mla_up",
    )(p, cosq, sinq, cosk, sink, q_norm.reshape(1, -1), kv_norm.reshape(1, -1), wqa, wqb, wk, wv, pk)


def _softmax_pv(s, v):
    m = jnp.max(s, axis=-1, keepdims=True)
    p = jnp.exp(s - m)
    l = jnp.sum(p, axis=-1, keepdims=True)
    return _dot(p.astype(BF16), v) / l


def _mla_attn_kernel(q_ref, k_ref, v_ref, o_ref, *, ctx):
    t = pl.program_id(2)
    q = q_ref[0]

    @pl.when(t == 0)
    def _():
        o_ref[0] = _softmax_pv(_dot_nt(q, k_ref[0, 0:ctx, :]), v_ref[0, 0:ctx, :]).astype(o_ref.dtype)

    @pl.when(t > 0)
    def _():
        o_ref[0] = _softmax_pv(_dot_nt(q, k_ref[0]), v_ref[0]).astype(o_ref.dtype)


def _mla_attention(q, k, v, ctx, tq):
    b, s, _ = q.shape
    return pl.pallas_call(
        functools.partial(_mla_attn_kernel, ctx=ctx),
        grid=(b, MLA_HEADS, s // tq),
        in_specs=[pl.BlockSpec((1, tq, MLA_HEAD_PAD), lambda i, h, t: (i, t, h)),
                  pl.BlockSpec((1, s, MLA_HEAD_PAD), lambda i, h, t: (i, 0, h)),
                  pl.BlockSpec((1, s, MLA_V), lambda i, h, t: (i, 0, h))],
        out_specs=pl.BlockSpec((1, tq, MLA_V), lambda i, h, t: (i, t, h)),
        out_shape=jax.ShapeDtypeStruct((b, s, MLA_HEADS * MLA_V), BF16),
        compiler_params=_params(("parallel", "parallel", "parallel")),
        name="mla_attention",
    )(q, k, v)


def _na_kernel(q_ref, k_ref, v_ref, bias_ref, o_ref, s0, s1, p0, p1, l0, l1, *, ctx, rows):
    w = GRID_W
    win = NA_KH * w
    s_scr, p_scr, l_scr = (s0, s1), (p0, p1), (l0, l1)
    lane = lax.broadcasted_iota(jnp.int32, (w, 2 * NA_DH), 1)
    first = lane < NA_DH
    kc = k_ref[0, 0:ctx, :]
    vc = v_ref[0, 0:ctx, :]

    def split_heads(qr, first_mask):
        zero = jnp.zeros_like(qr)
        return jnp.concatenate([jnp.where(first_mask, qr, zero), jnp.where(first_mask, zero, qr)], axis=0)

    def merge_heads(o2):
        n = o2.shape[0] // 2
        lane_n = lax.broadcasted_iota(jnp.int32, (n, 2 * NA_DH), 1)
        return jnp.where(lane_n < NA_DH, o2[:n], o2[n:])

    lane_c = lax.broadcasted_iota(jnp.int32, (ctx, 2 * NA_DH), 1)
    qc2 = split_heads(q_ref[0, 0:ctx, :], lane_c < NA_DH)
    o_ref[0, 0:ctx, :] = merge_heads(_softmax_pv(_dot_nt(qc2, kc), vc)).astype(o_ref.dtype)

    def window_start(r):
        return pl.multiple_of(ctx + jnp.clip(r - NA_KH // 2, 0, rows - NA_KH) * w, w)

    def scores(r, slot):
        pat = jnp.where(r < NA_KH // 2, r,
                        jnp.where(r <= rows - NA_KH // 2, NA_KH // 2, r - (rows - NA_KH)))
        q2 = split_heads(q_ref[0, pl.ds(pl.multiple_of(ctx + r * w, w), w), :], first)
        kw = k_ref[0, pl.ds(window_start(r), win), :]
        s_scr[slot][:, 0:win] = _dot_nt(q2, kw) + bias_ref[0, pat]
        s_scr[slot][:, win:win + ctx] = _dot_nt(q2, kc)

    def softmax(slot):
        s = s_scr[slot][...]
        m = jnp.max(s, axis=-1, keepdims=True)
        p = jnp.exp(s - m)
        l_scr[slot][...] = jnp.broadcast_to(jnp.sum(p, axis=-1, keepdims=True), (2 * w, 2 * NA_DH))
        p_scr[slot][...] = p.astype(BF16)

    def values(r, slot):
        vw = v_ref[0, pl.ds(window_start(r), win), :]
        o2 = (_dot(p_scr[slot][:, 0:win], vw) + _dot(p_scr[slot][:, win:win + ctx], vc)) / l_scr[slot][...]
        o_ref[0, pl.ds(pl.multiple_of(ctx + r * w, w), w), :] = merge_heads(o2).astype(o_ref.dtype)

    scores(0, 0)
    scores(1, 1)
    softmax(0)

    def two_rows(u, carry):
        t = 2 * u
        scores(t, 0)
        softmax(1)
        values(t - 2, 0)
        scores(t + 1, 1)
        softmax(0)
        values(t - 1, 1)
        return carry

    lax.fori_loop(1, rows // 2, two_rows, 0)
    softmax(1)
    values(rows - 2, 0)
    values(rows - 1, 1)


def _na_attention(q, k, v, bias, ctx):
    b, s, width = q.shape
    rows = (s - ctx) // GRID_W
    assert rows % 2 == 0
    pairs = width // (2 * NA_DH)
    keys = NA_KH * GRID_W + ctx
    blk = pl.BlockSpec((1, s, 2 * NA_DH), lambda h, i: (i, 0, h))
    scratch = ([pltpu.VMEM((2 * GRID_W, keys), F32)] * 2 + [pltpu.VMEM((2 * GRID_W, keys), BF16)] * 2
               + [pltpu.VMEM((2 * GRID_W, 2 * NA_DH), F32)] * 2)
    return pl.pallas_call(
        functools.partial(_na_kernel, ctx=ctx, rows=rows),
        grid=(pairs, b),
        in_specs=[blk, blk, blk,
                  pl.BlockSpec((1, NA_PATTERNS, 2 * GRID_W, NA_KH * GRID_W), lambda h, i: (h, 0, 0, 0))],
        out_specs=blk,
        out_shape=jax.ShapeDtypeStruct((b, s, width), BF16),
        scratch_shapes=scratch,
        compiler_params=_params(("parallel", "parallel")),
        name="na_attention",
    )(q, k, v, bias)


def _na_bias_kernel(rpb_ref, o_ref, *, rows):
    w, kh, kw = GRID_W, NA_KH, NA_KW
    pair = pl.program_id(0)
    qi = lax.broadcasted_iota(jnp.int32, (w, w), 0)
    ki = lax.broadcasted_iota(jnp.int32, (w, w), 1)
    col_idx = jnp.clip(ki - qi, -(kw - 1), kw - 1) + kw - 1
    c_start = jnp.clip(qi - kw // 2, 0, w - kw)
    masked = jnp.where((ki >= c_start) & (ki < c_start + kw), 0.0, NEG_BIG)
    rep = (0, 1, 2, 3, 4, rows - 3, rows - 2, rows - 1)
    for hh in range(2):
        head = 2 * pair + hh
        tiles = []
        for dr in range(2 * kh - 1):
            t = masked
            for dc in range(2 * kw - 1):
                t = jnp.where(col_idx == dc, masked + rpb_ref[head, dr, dc], t)
            tiles.append(t)
        for p, r in enumerate(rep):
            r_start = min(max(r - kh // 2, 0), rows - kh)
            for j in range(kh):
                o_ref[0, p, hh * w:(hh + 1) * w, j * w:(j + 1) * w] = tiles[r_start + j - r + kh - 1]


def _na_bias_table(rpb, rows):
    h = rpb.shape[0]
    shape = (h // 2, NA_PATTERNS, 2 * GRID_W, NA_KH * GRID_W)
    return pl.pallas_call(
        functools.partial(_na_bias_kernel, rows=rows),
        grid=(h // 2,),
        in_specs=[pl.BlockSpec(memory_space=pltpu.SMEM)],
        out_specs=pl.BlockSpec((1,) + shape[1:], lambda i: (i, 0, 0, 0)),
        out_shape=jax.ShapeDtypeStruct(shape, F32),
        compiler_params=_params(("parallel",)),
        name="na_bias_table",
    )(rpb.astype(F32))


def _gdn_kernel(q_ref, k_ref, v_ref, z_ref, ba_ref, cwq_ref, cwk_ref, cwv_ref, alog_ref, dtb_ref, gain_ref,
                bd_ref, tril_ref, triu_ref, o_ref, qs, ks, vs, wq_s, ut_s, qkk_s, gl_s, o_s, *, ctx, seq, hp):
    c = GDN_CHUNK
    dk = GDN_DK
    grp = GDN_GROUP
    gr = grp * c
    nc = seq // c
    ncc = ctx // c
    ng = seq // gr

    row = lax.broadcasted_iota(jnp.int32, (seq, 128), 0)
    pad = SHORT_CONV // 2
    tap_ok = {d: (row + d >= 0) & (row + d < seq) & ((row < ctx) == (row + d < ctx))
              for d in range(-pad, pad + 1) if d != 0}

    def short_conv(x, cw):
        acc = x * cw[pad:pad + 1, :]
        for j in range(SHORT_CONV):
            d = j - pad
            if d != 0:
                acc = acc + jnp.where(tap_ok[d], pltpu.roll(x, (-d) % seq, axis=0), 0.0) * cw[j:j + 1, :]
        return _silu(acc)

    def l2n(x):
        return x * lax.rsqrt(jnp.sum(x * x, axis=-1, keepdims=True) + EPS)

    w8 = 2 * gr
    sup = max(m for m in (1, 2, 3) if ng % m == 0)
    wide = sup * w8
    ii = lax.broadcasted_iota(jnp.int32, (c, wide), 0)
    lane8 = lax.broadcasted_iota(jnp.int32, (c, wide), 1)
    jj = lane8 & (c - 1)
    fwd8 = ((lane8 >> 6) & 1) == 0
    lchunk = lax.broadcasted_iota(jnp.int32, (c, w8), 1) >> 7
    incl = (fwd8 & (ii >= jj)) | (jnp.logical_not(fwd8) & (ii <= jj))
    strict = incl & (ii != jj)
    eye8 = (ii == jj).astype(F32)
    same8 = (ii >> 3) == (jj >> 3)
    lane_g = lax.broadcasted_iota(jnp.int32, (gr, 128), 1)
    fwd_half = lane_g < c
    fwd_half_c = lax.broadcasted_iota(jnp.int32, (c, 128), 1) < c

    def bd(y):
        yb = y.astype(BF16)
        return [jnp.concatenate([yb[:, h * gr:(h + 1) * gr]] * grp, axis=0) * bd_ref[...]
                for h in range(wide // gr)]

    def mm(x, ybd):
        xb = x.astype(BF16)
        return jnp.concatenate([_dot(xb[:, h * gr:(h + 1) * gr], ybd[h]) for h in range(wide // gr)], axis=1)

    def diag8(f):
        out = f[0:c]
        for b in range(1, grp):
            out = jnp.where(lchunk == b, f[b * c:(b + 1) * c], out)
        return out

    def inverse8(a):
        p = jnp.where(same8, -a, 0.0)
        t = eye8 + p
        p2 = mm(p, bd(p))
        r = mm(jnp.concatenate([t, p2], axis=0), bd(p2))
        t = t + r[0:c]
        t = t + mm(t, bd(r[c:2 * c]))
        for sh in (3, 4, 5):
            off = ((ii >> (sh + 1)) == (jj >> (sh + 1))) & ((ii >> sh) != (jj >> sh))
            l_mat = jnp.where(off, a, 0.0)
            t = t - mm(t, bd(mm(l_mat, bd(t))))
        return t

    def chunk_sums(x):
        hi = x.astype(BF16)
        r1 = x - hi.astype(F32)
        mid = r1.astype(BF16)
        lo = (r1 - mid.astype(F32)).astype(BF16)
        parts = jnp.concatenate([hi, mid, lo], axis=1)
        pre = _dot(tril_ref[...], parts)
        suf = _dot(triu_ref[...], parts)
        fold = lambda r: r[:, 0:128] + r[:, 128:256] + r[:, 256:384]
        return fold(pre), fold(suf)

    def pick(x, lane_id):
        col = jnp.sum(jnp.where(lane_g == lane_id, x, 0.0), axis=-1, keepdims=True)
        return jnp.broadcast_to(col, (gr, 128))

    def chunks(x):
        return [x[b * c:(b + 1) * c] for b in range(grp)]

    def across(xf, xb):
        both = jnp.where(fwd_half, xf, xb)
        return jnp.concatenate(chunks(both), axis=1)

    def interleave_rows(xf, xb):
        return jnp.concatenate([s for pair in zip(chunks(xf), chunks(xb)) for s in pair], axis=0)

    def group_inputs(gi, head):
        r0 = pl.multiple_of(gi * gr, gr)
        q = qs[pl.ds(r0, gr), :]
        k = ks[pl.ds(r0, gr), :]
        v = vs[pl.ds(r0, gr), :]
        ba = ba_ref[0, pl.ds(r0, gr), :]
        beta_all = jax.nn.sigmoid(ba)
        g_all = -jnp.exp(alog_ref[...]) * _softplus(ba + dtb_ref[...])
        pre_all, suf_all = chunk_sums(g_all)
        lane_f = 2 * GDN_HEADS + head
        lane_b = 3 * GDN_HEADS + head
        beta = (pick(beta_all, head), pick(beta_all, GDN_HEADS + head))
        gc = (pick(pre_all, lane_f), pick(suf_all, lane_b))
        grest = (pick(suf_all - g_all, lane_f), pick(pre_all - g_all, lane_b))
        kb = k.astype(BF16)
        kdup = interleave_rows(kb, kb)
        r = _dot_nt(jnp.concatenate([q.astype(BF16), kb], axis=0), kdup)
        g_j = interleave_rows(gc[0], gc[1]).T[0:c]
        return dict(q=q, k=k, v=v, beta=beta, gc=gc, grest=grest, qk8=diag8(r[0:gr]), kk8=diag8(r[gr:2 * gr]),
                    diff=across(gc[0], gc[1]) - g_j, beta8=across(beta[0], beta[1]))

    def prep(si, carry, head, hh):
        groups = [group_inputs(si * sup + g, head) for g in range(sup)]
        cat = lambda name: jnp.concatenate([g[name] for g in groups], axis=1)
        decay = jnp.where(incl, jnp.exp(jnp.minimum(cat("diff"), 0.0)), 0.0)
        t_all = inverse8(jnp.where(strict, cat("beta8") * cat("kk8") * decay, 0.0))
        qkd_all = cat("qk8") * decay
        for gidx, grp_in in enumerate(groups):
            q, k, v, beta, gc, grest = (grp_in[name] for name in ("q", "k", "v", "beta", "gc", "grest"))
            t8 = t_all[:, gidx * w8:(gidx + 1) * w8]
            qkd8 = qkd_all[:, gidx * w8:(gidx + 1) * w8]
            rhs, qd, kd, eg = [], [], [], []
            for d in range(2):
                e = jnp.exp(gc[d])
                eg.append(e)
                rhs.append(jnp.concatenate([(k * (beta[d] * e)).astype(BF16), (v * beta[d]).astype(BF16)], axis=1))
                qd.append((q * e).astype(BF16))
                kd.append(k * jnp.exp(grest[d]))
            for b in range(grp):
                n = (si * sup + gidx) * grp + b
                rows = slice(b * c, (b + 1) * c)
                tb = t8[:, 2 * c * b:2 * c * (b + 1)]
                lhs = jnp.concatenate([jnp.where(fwd_half_c, tb, 0.0), jnp.where(fwd_half_c, 0.0, tb)], axis=0)
                sol = _dot(lhs.astype(BF16), jnp.concatenate([rhs[0][rows], rhs[1][rows]], axis=0))
                for d in range(2):
                    sd = sol[d * c:(d + 1) * c]
                    wq_s[2 * hh + d, n] = jnp.concatenate([sd[:, 0:dk].astype(BF16), qd[d][rows]], axis=0)
                    ut_s[2 * hh + d, n] = sd[:, dk:]
                    lo = 2 * c * b + c * d
                    qkk_s[2 * hh + d, n] = jnp.concatenate([qkd8[:, lo:lo + c], kd[d][rows].T], axis=0).astype(BF16)
                    edge = b * c + c - 1 if d == 0 else b * c
                    gl_s[2 * hh + d, n] = jnp.broadcast_to(eg[d][edge:edge + 1, :], (8, 128))
        return carry

    for hh in range(hp):
        head = pl.program_id(1) * hp + hh
        ls = slice(hh * 128, (hh + 1) * 128)
        qs[...] = l2n(short_conv(q_ref[0, :, ls].astype(F32), cwq_ref[:, ls])) * (dk ** -0.5)
        ks[...] = l2n(short_conv(k_ref[0, :, ls].astype(F32), cwk_ref[:, ls]))
        vs[...] = short_conv(v_ref[0, :, ls].astype(F32), cwv_ref[:, ls])
        lax.fori_loop(0, ng // sup, functools.partial(prep, head=head, hh=hh), 0)

    def step(i, states):
        n_bwd = jnp.where(i < ncc, ncc - 1 - i, nc - 1 - (i - ncc))
        new = []
        for ch in range(2 * hp):
            n = i if ch % 2 == 0 else n_bwd
            state = states[ch]
            sb = state.astype(BF16)
            r1 = _dot(wq_s[ch, n], sb)
            ub = (ut_s[ch, n] - r1[0:c]).astype(BF16)
            r2 = _dot(qkk_s[ch, n], ub)
            o_s[ch, pl.ds(pl.multiple_of(n * c, c), c), :] = r1[c:2 * c] + r2[0:c]
            new.append(gl_s[ch, n][0:1, :] * state + r2[c:c + dk])
        return tuple(new)

    zero = jnp.zeros((dk, GDN_DV), F32)
    lax.fori_loop(0, nc, step, (zero,) * (2 * hp))

    for hh in range(hp):
        ls = slice(hh * 128, (hh + 1) * 128)
        o = _rms(o_s[2 * hh] + o_s[2 * hh + 1], gain_ref[...]) * _silu(z_ref[0, :, ls].astype(F32))
        o_ref[0, :, ls] = o.astype(o_ref.dtype)


def _gdn(qkv, z, ba, conv_w, alog_row, dtb_row, gain, ctx):
    b, s, _ = qkv.shape
    h = GDN_HEADS
    c = GDN_CHUNK
    assert s % (GDN_GROUP * c) == 0
    nc = s // c
    gr = GDN_GROUP * c
    blk = lambda n: jnp.arange(n) // c
    bd_ones = (blk(gr)[:, None] == blk(gr)[None, :]).astype(BF16)
    same = blk(gr)[:, None] == blk(gr)[None, :]
    tri_lo = (same & (jnp.arange(gr)[:, None] >= jnp.arange(gr)[None, :])).astype(BF16)
    tri_up = (same & (jnp.arange(gr)[:, None] <= jnp.arange(gr)[None, :])).astype(BF16)
    hp = GDN_HEADS_PER_STEP
    lanes = hp * 128
    col = lambda off: pl.BlockSpec((1, s, lanes), lambda i, j: (i, 0, off // hp + j))
    cw = lambda off: pl.BlockSpec((SHORT_CONV, lanes), lambda i, j: (0, off // hp + j))
    ch = 2 * hp
    scratch = [pltpu.VMEM((s, 128), F32), pltpu.VMEM((s, 128), F32), pltpu.VMEM((s, 128), F32),
               pltpu.VMEM((ch, nc, 2 * c, 128), BF16), pltpu.VMEM((ch, nc, c, 128), F32),
               pltpu.VMEM((ch, nc, c + GDN_DK, c), BF16), pltpu.VMEM((ch, nc, 8, 128), F32),
               pltpu.VMEM((ch, s, 128), F32)]
    return pl.pallas_call(
        functools.partial(_gdn_kernel, ctx=ctx, seq=s, hp=hp),
        grid=(b, h // hp),
        in_specs=[col(0), col(h), col(2 * h),
                  pl.BlockSpec((1, s, lanes), lambda i, j: (i, 0, j)),
                  pl.BlockSpec((1, s, 128), lambda i, j: (i, 0, 0)),
                  cw(0), cw(h), cw(2 * h),
                  _const_spec((1, 128)), _const_spec((1, 128)), _const_spec((1, 128)),
                  _const_spec(bd_ones.shape), _const_spec(tri_lo.shape), _const_spec(tri_up.shape)],
        out_specs=pl.BlockSpec((1, s, lanes), lambda i, j: (i, 0, j)),
        out_shape=jax.ShapeDtypeStruct((b, s, h * GDN_DV), BF16),
        scratch_shapes=scratch,
        compiler_params=_params(("parallel", "parallel")),
        name="gdn",
    )(qkv, qkv, qkv, z, ba, conv_w, conv_w, conv_w, alog_row, dtb_row, gain.reshape(1, -1), bd_ones, tri_lo, tri_up)


def _rope_tables(n_lat, ctx):
    t = jnp.arange(n_lat)
    row = (t // GRID_W).astype(F32)
    col = (t % GRID_W).astype(F32)
    quarter = MLA_ROPE // 4
    inv_freq = ROPE_BASE ** (-jnp.arange(quarter, dtype=F32) / quarter)
    ang_r = row[:, None] * inv_freq
    ang_c = col[:, None] * inv_freq
    ang = jnp.concatenate([ang_r, ang_r, ang_c, ang_c], axis=-1)
    cos = jnp.concatenate([jnp.ones((ctx, MLA_ROPE), F32), jnp.cos(ang)], axis=0)
    sin = jnp.concatenate([jnp.zeros((ctx, MLA_ROPE), F32), jnp.sin(ang)], axis=0)
    s = ctx + n_lat
    one = jnp.ones((s, MLA_NOPE), F32)
    zn = jnp.zeros((s, MLA_NOPE), F32)
    zp = jnp.zeros((s, MLA_HEAD_PAD - MLA_NOPE - MLA_ROPE), F32)
    cosq = jnp.concatenate([one, cos, zp], axis=1)
    sinq = jnp.concatenate([zn, sin, zp], axis=1)
    zk = jnp.zeros((s, 128 - MLA_ROPE), F32)
    cosk = jnp.concatenate([cos, zk], axis=1)
    sink = jnp.concatenate([sin, zk], axis=1)
    return cosq, sinq, cosk, sink


def _rot_cols(w):
    q = MLA_ROPE // 4
    a, b, c, d = w[..., 0:q], w[..., q:2 * q], w[..., 2 * q:3 * q], w[..., 3 * q:4 * q]
    return jnp.concatenate([-b, a, -d, c], axis=-1)


def _even_weights(w_in, w_uq, w_ukv, w_out):
    d = w_in.shape[0]
    h = GDN_HEADS
    qkv_w = 2 * h * GDN_DK + h * GDN_DV
    vw = h * GDN_DV
    o = 0
    w_qkv = w_in[:, o:o + qkv_w]; o += qkv_w
    w_z = w_in[:, o:o + vw]; o += vw
    w_ba = w_in[:, o:o + 4 * h]; o += 4 * h
    w_cq = w_in[:, o:o + MLA_Q_RANK]; o += MLA_Q_RANK
    w_ckv = w_in[:, o:o + MLA_KV_RANK]; o += MLA_KV_RANK
    w_kr = w_in[:, o:o + MLA_ROPE]
    z64 = jnp.zeros((d, 128 - MLA_ROPE), w_in.dtype)
    w_ba = jnp.concatenate([w_ba, jnp.zeros((d, 128 - 4 * h), w_in.dtype)], axis=1)
    w_mla = jnp.concatenate([w_cq, w_ckv, w_kr, z64, _rot_cols(w_kr), z64], axis=1)
    proj = [w_qkv.astype(BF16), w_z.astype(BF16), w_ba.astype(BF16), w_mla.astype(BF16)]

    rq = MLA_Q_RANK
    wq = w_uq.reshape(rq, MLA_HEADS, MLA_NOPE + MLA_ROPE)
    zpad = jnp.zeros((rq, MLA_HEADS, MLA_HEAD_PAD - MLA_NOPE - MLA_ROPE), w_uq.dtype)
    znope = jnp.zeros((rq, MLA_HEADS, MLA_NOPE), w_uq.dtype)
    wqa = jnp.concatenate([wq, zpad], axis=-1).reshape(rq, -1)
    wqb = jnp.concatenate([znope, _rot_cols(wq[..., MLA_NOPE:]), zpad], axis=-1).reshape(rq, -1)
    rk = MLA_KV_RANK
    wkv = w_ukv.reshape(rk, MLA_HEADS, MLA_NOPE + MLA_V)
    wk = jnp.concatenate([wkv[..., :MLA_NOPE], jnp.zeros((rk, MLA_HEADS, MLA_HEAD_PAD - MLA_NOPE), w_ukv.dtype)],
                         axis=-1).reshape(rk, -1)
    wv = wkv[..., MLA_NOPE:].reshape(rk, -1)
    eye = jnp.eye(MLA_ROPE, dtype=F32)
    blk = jnp.concatenate([jnp.zeros((MLA_ROPE, MLA_NOPE), F32), eye,
                           jnp.zeros((MLA_ROPE, MLA_HEAD_PAD - MLA_NOPE - MLA_ROPE), F32)], axis=1)
    pk = jnp.concatenate([jnp.tile(blk, (1, MLA_HEADS)),
                          jnp.zeros((128 - MLA_ROPE, MLA_HEADS * MLA_HEAD_PAD), F32)], axis=0)
    up = [wqa.astype(BF16), wqb.astype(BF16), wk.astype(BF16), wv.astype(BF16), pk.astype(BF16)]
    out = [w_out[:vw].astype(BF16), w_out[vw:].astype(BF16)]
    return proj, up, out


def _gate_rows(a_log, dt_bias):
    n = 2 * GDN_HEADS
    z = jnp.zeros((n,), F32)
    tail = jnp.zeros((128 - 2 * n,), F32)
    alog_row = jnp.concatenate([z, a_log.astype(F32).reshape(n), tail]).reshape(1, 128)
    dtb_row = jnp.concatenate([z, dt_bias.astype(F32).reshape(n), tail]).reshape(1, 128)
    return alog_row, dtb_row


def kernel(x, c, ctx, c_ctx, ada_w, ada_b, norm_mix_g, norm_ffn_g, ev_w_in, ev_conv_qkv, ev_a_log, ev_dt_bias,
           ev_gdn_norm, ev_q_norm, ev_kv_norm, ev_w_uq, ev_w_ukv, ev_w_out, od_w_qkv, od_rpb, od_w_out,
           ffn_w_in, ffn_conv, ffn_conv_b, ffn_w_out, final_g):
    b, n_lat, d = x.shape
    n_ctx = ctx.shape[1]
    depth = ada_w.shape[0]
    tm = n_ctx
    assert n_lat % tm == 0 and n_lat % GRID_W == 0 and tm % GDN_CHUNK == 0
    rows = n_lat // GRID_W
    assert rows >= NA_KH

    r_pad = -(b + 1) % 8
    cond = jnp.concatenate([c, c_ctx[None, :], jnp.zeros((r_pad, d), c.dtype)], axis=0)
    mods = _ada_table(cond, ada_w, ada_b).reshape(depth, b + 1 + r_pad, 6, d)
    tabs = _rope_tables(n_lat, n_ctx)
    d_ff = ffn_w_out.shape[1]

    xa = jnp.concatenate([ctx, x], axis=1)
    for i in range(depth):
        j = i // 2
        m = mods[i]
        if i % 2 == 0:
            proj_w, up_w, out_w = _even_weights(ev_w_in[j], ev_w_uq[j], ev_w_ukv[j], ev_w_out[j])
            qkv, z, ba, p_mla = _norm_mod_matmul(xa, m, norm_mix_g[i], proj_w, [BF16, BF16, F32, F32], 0, tm)
            alog_row, dtb_row = _gate_rows(ev_a_log[j], ev_dt_bias[j])
            y_a = _gdn(qkv, z, ba, ev_conv_qkv[j], alog_row, dtb_row, ev_gdn_norm[j], n_ctx)
            q, k, v = _mla_up(p_mla, tabs, ev_q_norm[j], ev_kv_norm[j], *up_w, tm)
            y_b = _mla_attention(q, k, v, n_ctx, tm)
            ys, w_outs = [y_a, y_b], out_w
        else:
            w = od_w_qkv[j]
            nw = w.shape[1] // 3
            ws = [(w[:, :nw] * (NA_DH ** -0.5)).astype(BF16), w[:, nw:2 * nw].astype(BF16),
                  w[:, 2 * nw:].astype(BF16)]
            q, k, v = _norm_mod_matmul(xa, m, norm_mix_g[i], ws, [BF16, BF16, BF16], 0, tm)
            ys, w_outs = [_na_attention(q, k, v, _na_bias_table(od_rpb[j], rows), n_ctx)], [od_w_out[j].astype(BF16)]
        wi = ffn_w_in[i]
        xa = _mix_ffn_residual(xa, m, ys, w_outs, norm_ffn_g[i], wi[:, :d_ff].astype(BF16),
                               wi[:, d_ff:].astype(BF16), ffn_conv[i], ffn_conv_b[i], ffn_w_out[i].astype(BF16), tm,
                               final_g=final_g if i == depth - 1 else None)
    return xa
```

```python
import functools
import math

import jax
import jax.numpy as jnp
from jax import lax
from jax.experimental import pallas as pl
from jax.experimental.pallas import tpu as pltpu

EPS = 1e-6
ROPE_BASE = 10000.0
GRID_W = 64

GDN_HEADS = 4
GDN_DK = 128
GDN_DV = 128
GDN_CHUNK = 64
GDN_GROUP = 4
GDN_HEADS_PER_STEP = 2
GDN_ZERO_ROWS = 8
SHORT_CONV = 5

MLA_HEADS = 4
MLA_NOPE = 128
MLA_ROPE = 64
MLA_V = 128
MLA_Q_RANK = 256
MLA_KV_RANK = 256
MLA_HEAD_PAD = 256

NA_HEADS = 16
NA_DH = 64
NA_KH = 8
NA_KW = 16
NA_PATTERNS = 8

FFN_CONV = 3
HALO = 16

VMEM_LIMIT = 56 * 1024 * 1024
NEG_BIG = -1e30

BF16 = jnp.bfloat16
F32 = jnp.float32


def _dot(a, b):
    return jnp.dot(a, b, preferred_element_type=F32)


def _dot_nt(a, b):
    return lax.dot_general(a, b, (((1,), (1,)), ((), ())), preferred_element_type=F32)


def _silu(x):
    return x * jax.nn.sigmoid(x)


def _softplus(x):
    return jnp.maximum(x, 0.0) + jnp.log(1.0 + jnp.exp(-jnp.abs(x)))


def _rms(x, g):
    return x * lax.rsqrt(jnp.mean(x * x, axis=-1, keepdims=True) + EPS) * g


def _params(sem):
    return pltpu.CompilerParams(dimension_semantics=sem, vmem_limit_bytes=VMEM_LIMIT)


def _const_spec(shape):
    nd = len(shape)
    return pl.BlockSpec(shape, lambda *_: (0,) * nd)


def _ada_kernel(c_ref, w_ref, b_ref, o_ref):
    s = _silu(c_ref[...]).astype(BF16)
    o_ref[0] = _dot(s, w_ref[0]) + b_ref[0]


def _ada_table(cond, ada_w, ada_b):
    depth, d, n = ada_w.shape
    r = cond.shape[0]
    tn = 1536
    return pl.pallas_call(
        _ada_kernel,
        grid=(depth, n // tn),
        in_specs=[pl.BlockSpec((r, d), lambda i, j: (0, 0)),
                  pl.BlockSpec((1, d, tn), lambda i, j: (i, 0, j)),
                  pl.BlockSpec((1, 1, tn), lambda i, j: (i, 0, j))],
        out_specs=pl.BlockSpec((1, r, tn), lambda i, j: (i, 0, j)),
        out_shape=jax.ShapeDtypeStruct((depth, r, n), F32),
        compiler_params=_params(("parallel", "parallel")),
        name="ada_table",
    )(cond, ada_w.astype(BF16), ada_b.reshape(depth, 1, n))


def _nmm_kernel(x_ref, mod_ref, g_ref, *rest, n_out, shift_idx):
    w_refs, o_refs = rest[:n_out], rest[n_out:]
    shift = mod_ref[0, shift_idx:shift_idx + 1, :]
    scale = mod_ref[0, shift_idx + 1:shift_idx + 2, :]
    h = _rms(x_ref[0], g_ref[...]) * (1.0 + scale) + shift
    hb = h.astype(BF16)
    for w_ref, o_ref in zip(w_refs, o_refs):
        o_ref[0] = _dot(hb, w_ref[...]).astype(o_ref.dtype)


def _norm_mod_matmul(xa, mods, g, ws, out_dtypes, shift_idx, tm):
    b, s, d = xa.shape
    n_out = len(ws)
    in_specs = [pl.BlockSpec((1, tm, d), lambda i, t: (i, t, 0)),
                pl.BlockSpec((1, 6, d), lambda i, t: (jnp.where(t == 0, b, i), 0, 0)),
                _const_spec((1, d))]
    in_specs += [_const_spec(w.shape) for w in ws]
    out_specs = [pl.BlockSpec((1, tm, w.shape[1]), lambda i, t: (i, t, 0)) for w in ws]
    out_shape = [jax.ShapeDtypeStruct((b, s, w.shape[1]), dt) for w, dt in zip(ws, out_dtypes)]
    return pl.pallas_call(
        functools.partial(_nmm_kernel, n_out=n_out, shift_idx=shift_idx),
        grid=(b, s // tm),
        in_specs=in_specs, out_specs=out_specs, out_shape=out_shape,
        compiler_params=_params(("parallel", "parallel")),
        name="norm_mod_matmul",
    )(xa, mods, g.reshape(1, d), *ws)


def _mix_ffn_kernel(x_ref, xp_ref, xn_ref, *rest, n_in, tm, n_tiles, t_off, final):
    y_refs = rest[:3 * n_in]
    w_refs = rest[3 * n_in:4 * n_in]
    mod_ref, g_ref, wu_ref, wg_ref, cw_ref, cb_ref, wo_ref, fg_ref, o_ref = rest[4 * n_in:]
    t = pl.program_id(1) + t_off
    ext = lambda main, prev, nxt: jnp.concatenate([prev[0], main[0], nxt[0]], axis=0)
    proj = _dot(ext(*y_refs[0:3]), w_refs[0][...])
    for j in range(1, n_in):
        proj = proj + _dot(ext(*y_refs[3 * j:3 * j + 3]), w_refs[j][...])
    x_ext = ext(x_ref, xp_ref, xn_ref) + mod_ref[0, 2:3, :] * proj
    x = x_ext[HALO:HALO + tm]
    shift = mod_ref[0, 3:4, :]
    scale = mod_ref[0, 4:5, :]
    gate = mod_ref[0, 5:6, :]
    h_ext = (_rms(x_ext, g_ref[...]) * (1.0 + scale) + shift).astype(BF16)
    gt = _dot(h_ext, wg_ref[...])
    u = _dot(h_ext[HALO:HALO + tm], wu_ref[...])
    prev_ok = (t >= 2).astype(F32)
    next_ok = jnp.logical_and(t >= 1, t < n_tiles - 1).astype(F32)
    row = lax.broadcasted_iota(jnp.int32, (tm + 2 * HALO, 1), 0)
    keep = jnp.where(row < HALO, prev_ok, jnp.where(row >= tm + HALO, next_ok, 1.0))
    gt = gt * keep
    g_prev = pltpu.roll(gt, 1, axis=0)[HALO:HALO + tm]
    g_next = pltpu.roll(gt, tm + 2 * HALO - 1, axis=0)[HALO:HALO + tm]
    g_mid = gt[HALO:HALO + tm]
    conv = g_prev * cw_ref[0:1, :] + g_mid * cw_ref[1:2, :] + g_next * cw_ref[2:3, :] + cb_ref[...]
    act = (_silu(conv) * u).astype(BF16)
    y = x + gate * _dot(act, wo_ref[...])
    o_ref[0] = _rms(y, fg_ref[...]) if final else y


def _mix_ffn_residual(xa, mods, ys, w_outs, g, wu, wg, conv_w, conv_b, wo, tm, final_g=None):
    b, s, d = xa.shape
    f = wu.shape[1]
    n_tiles = s // tm
    hb = tm // HALO
    last = s // HALO - 1
    final = final_g is not None
    t_off = 1 if final else 0
    fg = final_g if final else g
    mod_row = (lambda i, t: (i, 0, 0)) if final else (lambda i, t: (jnp.where(t == 0, b, i), 0, 0))
    main = lambda w: pl.BlockSpec((1, tm, w), lambda i, t: (i, t + t_off, 0))
    prev = lambda w: pl.BlockSpec((1, HALO, w), lambda i, t: (i, jnp.maximum((t + t_off) * hb - 1, 0), 0))
    nxt = lambda w: pl.BlockSpec((1, HALO, w), lambda i, t: (i, jnp.minimum((t + t_off + 1) * hb, last), 0))
    in_specs = [main(d), prev(d), nxt(d)]
    args = [xa, xa, xa]
    for y in ys:
        in_specs += [main(y.shape[2]), prev(y.shape[2]), nxt(y.shape[2])]
        args += [y, y, y]
    in_specs += [_const_spec(w.shape) for w in w_outs]
    in_specs += [pl.BlockSpec((1, 6, d), mod_row), _const_spec((1, d)), _const_spec((d, f)), _const_spec((d, f)),
                 _const_spec((FFN_CONV, f)), _const_spec((1, f)), _const_spec((f, d)), _const_spec((1, d))]
    args += list(w_outs) + [mods, g.reshape(1, d), wu, wg, conv_w, conv_b.reshape(1, f), wo, fg.reshape(1, d)]
    return pl.pallas_call(
        functools.partial(_mix_ffn_kernel, n_in=len(ys), tm=tm, n_tiles=n_tiles, t_off=t_off, final=final),
        grid=(b, n_tiles - t_off),
        in_specs=in_specs,
        out_specs=pl.BlockSpec((1, tm, d), lambda i, t: (i, t, 0)),
        out_shape=jax.ShapeDtypeStruct((b, s - t_off * tm, d), F32),
        compiler_params=_params(("parallel", "parallel")),
        name="mix_ffn",
    )(*args)


def _mla_up_kernel(p_ref, cq_ref, sq_ref, ck_ref, sk_ref, qn_ref, kvn_ref,
                   wqa_ref, wqb_ref, wk_ref, wv_ref, pk_ref, q_ref, k_ref, v_ref, *, scale):
    r = MLA_Q_RANK
    cq = _rms(p_ref[0, :, 0:r], qn_ref[...]).astype(BF16)
    ckv = _rms(p_ref[0, :, r:2 * r], kvn_ref[...]).astype(BF16)
    kr = p_ref[0, :, 2 * r:2 * r + 128]
    krr = p_ref[0, :, 2 * r + 128:2 * r + 256]
    k_rope = (kr * ck_ref[...] + krr * sk_ref[...]).astype(BF16)
    k_ref[0] = (_dot(ckv, wk_ref[...]) + _dot(k_rope, pk_ref[...])).astype(BF16)
    v_ref[0] = _dot(ckv, wv_ref[...]).astype(BF16)
    qa = _dot(cq, wqa_ref[...])
    qb = _dot(cq, wqb_ref[...])
    cq_t = cq_ref[...] * scale
    sq_t = sq_ref[...] * scale
    for h in range(MLA_HEADS):
        sl = slice(h * MLA_HEAD_PAD, (h + 1) * MLA_HEAD_PAD)
        q_ref[0, :, sl] = (qa[:, sl] * cq_t + qb[:, sl] * sq_t).astype(BF16)


def _mla_up(p, tabs, q_norm, kv_norm, wqa, wqb, wk, wv, pk, tm):
    b, s, pw = p.shape
    cosq, sinq, cosk, sink = tabs
    hw = MLA_HEADS * MLA_HEAD_PAD
    vw = MLA_HEADS * MLA_V
    scale = (MLA_NOPE + MLA_ROPE) ** -0.5
    row = lambda w: pl.BlockSpec((tm, w), lambda i, t: (t, 0))
    return pl.pallas_call(
        functools.partial(_mla_up_kernel, scale=scale),
        grid=(b, s // tm),
        in_specs=[pl.BlockSpec((1, tm, pw), lambda i, t: (i, t, 0)),
                  row(MLA_HEAD_PAD), row(MLA_HEAD_PAD), row(128), row(128),
                  _const_spec((1, MLA_Q_RANK)), _const_spec((1, MLA_KV_RANK)),
                  _const_spec(wqa.shape), _const_spec(wqb.shape), _const_spec(wk.shape),
                  _const_spec(wv.shape), _const_spec(pk.shape)],
        out_specs=[pl.BlockSpec((1, tm, hw), lambda i, t: (i, t, 0)),
                   pl.BlockSpec((1, tm, hw), lambda i, t: (i, t, 0)),
                   pl.BlockSpec((1, tm, vw), lambda i, t: (i, t, 0))],
        out_shape=[jax.ShapeDtypeStruct((b, s, hw), BF16),
                   jax.ShapeDtypeStruct((b, s, hw), BF16),
                   jax.ShapeDtypeStruct((b, s, vw), BF16)],
        compiler_params=_params(("parallel", "parallel")),
        name="mla_up",
    )(p, cosq, sinq, cosk, sink, q_norm.reshape(1, -1), kv_norm.reshape(1, -1), wqa, wqb, wk, wv, pk)


def _softmax_pv(s, v):
    m = jnp.max(s, axis=-1, keepdims=True)
    p = jnp.exp(s - m)
    l = jnp.sum(p, axis=-1, keepdims=True)
    return _dot(p.astype(BF16), v) / l


def _mla_attn_kernel(q_ref, k_ref, v_ref, o_ref, *, ctx):
    t = pl.program_id(2)
    q = q_ref[0]

    @pl.when(t == 0)
    def _():
        o_ref[0] = _softmax_pv(_dot_nt(q, k_ref[0, 0:ctx, :]), v_ref[0, 0:ctx, :]).astype(o_ref.dtype)

    @pl.when(t > 0)
    def _():
        o_ref[0] = _softmax_pv(_dot_nt(q, k_ref[0]), v_ref[0]).astype(o_ref.dtype)


def _mla_attention(q, k, v, ctx, tq):
    b, s, _ = q.shape
    return pl.pallas_call(
        functools.partial(_mla_attn_kernel, ctx=ctx),
        grid=(b, MLA_HEADS, s // tq),
        in_specs=[pl.BlockSpec((1, tq, MLA_HEAD_PAD), lambda i, h, t: (i, t, h)),
                  pl.BlockSpec((1, s, MLA_HEAD_PAD), lambda i, h, t: (i, 0, h)),
                  pl.BlockSpec((1, s, MLA_V), lambda i, h, t: (i, 0, h))],
        out_specs=pl.BlockSpec((1, tq, MLA_V), lambda i, h, t: (i, t, h)),
        out_shape=jax.ShapeDtypeStruct((b, s, MLA_HEADS * MLA_V), BF16),
        compiler_params=_params(("parallel", "parallel", "parallel")),
        name="mla_attention",
    )(q, k, v)


def _na_kernel(q_ref, k_ref, v_ref, bias_ref, o_ref, s0, s1, p0, p1, l0, l1, *, ctx, rows):
    w = GRID_W
    win = NA_KH * w
    s_scr, p_scr, l_scr = (s0, s1), (p0, p1), (l0, l1)
    lane = lax.broadcasted_iota(jnp.int32, (w, 2 * NA_DH), 1)
    first = lane < NA_DH
    kc = k_ref[0, 0:ctx, :]
    vc = v_ref[0, 0:ctx, :]

    def split_heads(qr, first_mask):
        zero = jnp.zeros_like(qr)
        return jnp.concatenate([jnp.where(first_mask, qr, zero), jnp.where(first_mask, zero, qr)], axis=0)

    def merge_heads(o2):
        n = o2.shape[0] // 2
        lane_n = lax.broadcasted_iota(jnp.int32, (n, 2 * NA_DH), 1)
        return jnp.where(lane_n < NA_DH, o2[:n], o2[n:])

    lane_c = lax.broadcasted_iota(jnp.int32, (ctx, 2 * NA_DH), 1)
    qc2 = split_heads(q_ref[0, 0:ctx, :], lane_c < NA_DH)
    o_ref[0, 0:ctx, :] = merge_heads(_softmax_pv(_dot_nt(qc2, kc), vc)).astype(o_ref.dtype)

    def window_start(r):
        return pl.multiple_of(ctx + jnp.clip(r - NA_KH // 2, 0, rows - NA_KH) * w, w)

    def scores(r, slot):
        pat = jnp.where(r < NA_KH // 2, r,
                        jnp.where(r <= rows - NA_KH // 2, NA_KH // 2, r - (rows - NA_KH)))
        q2 = split_heads(q_ref[0, pl.ds(pl.multiple_of(ctx + r * w, w), w), :], first)
        kw = k_ref[0, pl.ds(window_start(r), win), :]
        s_scr[slot][:, 0:win] = _dot_nt(q2, kw) + bias_ref[0, pat]
        s_scr[slot][:, win:win + ctx] = _dot_nt(q2, kc)

    def softmax(slot):
        s = s_scr[slot][...]
        m = jnp.max(s, axis=-1, keepdims=True)
        p = jnp.exp(s - m)
        l_scr[slot][...] = jnp.broadcast_to(jnp.sum(p, axis=-1, keepdims=True), (2 * w, 2 * NA_DH))
        p_scr[slot][...] = p.astype(BF16)

    def values(r, slot):
        vw = v_ref[0, pl.ds(window_start(r), win), :]
        o2 = (_dot(p_scr[slot][:, 0:win], vw) + _dot(p_scr[slot][:, win:win + ctx], vc)) / l_scr[slot][...]
        o_ref[0, pl.ds(pl.multiple_of(ctx + r * w, w), w), :] = merge_heads(o2).astype(o_ref.dtype)

    scores(0, 0)
    scores(1, 1)
    softmax(0)

    def two_rows(u, carry):
        t = 2 * u
        scores(t, 0)
        softmax(1)
        values(t - 2, 0)
        scores(t + 1, 1)
        softmax(0)
        values(t - 1, 1)
        return carry

    lax.fori_loop(1, rows // 2, two_rows, 0)
    softmax(1)
    values(rows - 2, 0)
    values(rows - 1, 1)


def _na_attention(q, k, v, bias, ctx):
    b, s, width = q.shape
    rows = (s - ctx) // GRID_W
    assert rows % 2 == 0
    pairs = width // (2 * NA_DH)
    keys = NA_KH * GRID_W + ctx
    blk = pl.BlockSpec((1, s, 2 * NA_DH), lambda h, i: (i, 0, h))
    scratch = ([pltpu.VMEM((2 * GRID_W, keys), F32)] * 2 + [pltpu.VMEM((2 * GRID_W, keys), BF16)] * 2
               + [pltpu.VMEM((2 * GRID_W, 2 * NA_DH), F32)] * 2)
    return pl.pallas_call(
        functools.partial(_na_kernel, ctx=ctx, rows=rows),
        grid=(pairs, b),
        in_specs=[blk, blk, blk,
                  pl.BlockSpec((1, NA_PATTERNS, 2 * GRID_W, NA_KH * GRID_W), lambda h, i: (h, 0, 0, 0))],
        out_specs=blk,
        out_shape=jax.ShapeDtypeStruct((b, s, width), BF16),
        scratch_shapes=scratch,
        compiler_params=_params(("parallel", "parallel")),
        name="na_attention",
    )(q, k, v, bias)


def _na_bias_kernel(rpb_ref, o_ref, *, rows):
    w, kh, kw = GRID_W, NA_KH, NA_KW
    pair = pl.program_id(0)
    qi = lax.broadcasted_iota(jnp.int32, (w, w), 0)
    ki = lax.broadcasted_iota(jnp.int32, (w, w), 1)
    col_idx = jnp.clip(ki - qi, -(kw - 1), kw - 1) + kw - 1
    c_start = jnp.clip(qi - kw // 2, 0, w - kw)
    masked = jnp.where((ki >= c_start) & (ki < c_start + kw), 0.0, NEG_BIG)
    rep = (0, 1, 2, 3, 4, rows - 3, rows - 2, rows - 1)
    for hh in range(2):
        head = 2 * pair + hh
        tiles = []
        for dr in range(2 * kh - 1):
            t = masked
            for dc in range(2 * kw - 1):
                t = jnp.where(col_idx == dc, masked + rpb_ref[head, dr, dc], t)
            tiles.append(t)
        for p, r in enumerate(rep):
            r_start = min(max(r - kh // 2, 0), rows - kh)
            for j in range(kh):
                o_ref[0, p, hh * w:(hh + 1) * w, j * w:(j + 1) * w] = tiles[r_start + j - r + kh - 1]


def _na_bias_table(rpb, rows):
    h = rpb.shape[0]
    shape = (h // 2, NA_PATTERNS, 2 * GRID_W, NA_KH * GRID_W)
    return pl.pallas_call(
        functools.partial(_na_bias_kernel, rows=rows),
        grid=(h // 2,),
        in_specs=[pl.BlockSpec(memory_space=pltpu.SMEM)],
        out_specs=pl.BlockSpec((1,) + shape[1:], lambda i: (i, 0, 0, 0)),
        out_shape=jax.ShapeDtypeStruct(shape, F32),
        compiler_params=_params(("parallel",)),
        name="na_bias_table",
    )(rpb.astype(F32))


def _gdn_kernel(q_ref, k_ref, v_ref, z_ref, ba_ref, cwq_ref, cwk_ref, cwv_ref, alog_ref, dtb_ref, gain_ref,
                bd_ref, tril_ref, triu_ref, o_ref, xpad, gate_s, qs, ks, vs, wu_s, qd_s, kq_s, gl_s, bq_s, a_s, c_s, o_s,
                *, ctx, seq, hp):
    c = GDN_CHUNK
    dk = GDN_DK
    grp = GDN_GROUP
    gr = grp * c
    nc = seq // c
    ncc = ctx // c
    ng = seq // gr

    pad = SHORT_CONV // 2
    zp = GDN_ZERO_ROWS

    def short_conv(x, cw):
        zeros = jnp.zeros((zp, 128), F32)
        xpad[0:zp] = zeros
        xpad[zp:zp + ctx] = x[0:ctx]
        xpad[zp + ctx:2 * zp + ctx] = zeros
        xpad[2 * zp + ctx:2 * zp + seq] = x[ctx:seq]
        xpad[2 * zp + seq:3 * zp + seq] = zeros

        def segment(base, n):
            acc = xpad[base - pad:base - pad + n] * cw[0:1, :]
            for j in range(1, SHORT_CONV):
                acc = acc + xpad[base - pad + j:base - pad + j + n] * cw[j:j + 1, :]
            return acc

        return _silu(jnp.concatenate([segment(zp, ctx), segment(2 * zp + ctx, seq - ctx)], axis=0))

    def l2n(x):
        return x * lax.rsqrt(jnp.sum(x * x, axis=-1, keepdims=True) + EPS)

    w8 = 2 * gr
    sup = max(m for m in (1, 2, 3) if ng % m == 0)
    wide = sup * w8
    ii = lax.broadcasted_iota(jnp.int32, (c, wide), 0)
    lane8 = lax.broadcasted_iota(jnp.int32, (c, wide), 1)
    jj = lane8 & (c - 1)
    fwd8 = ((lane8 >> 6) & 1) == 0
    lchunk = lax.broadcasted_iota(jnp.int32, (c, w8), 1) >> 7
    incl = (fwd8 & (ii >= jj)) | (jnp.logical_not(fwd8) & (ii <= jj))
    strict = incl & (ii != jj)
    eye8 = (ii == jj).astype(F32)
    same8 = (ii >> 3) == (jj >> 3)
    lane_g = lax.broadcasted_iota(jnp.int32, (gr, 128), 1)
    fwd_half = lane_g < c
    fwd_half_c = lax.broadcasted_iota(jnp.int32, (c, 128), 1) < c

    def bd(y):
        yb = y.astype(BF16)
        return [jnp.concatenate([yb[:, h * gr:(h + 1) * gr]] * grp, axis=0) * bd_ref[...]
                for h in range(wide // gr)]

    def mm(x, ybd):
        xb = x.astype(BF16)
        return jnp.concatenate([_dot(xb[:, h * gr:(h + 1) * gr], ybd[h]) for h in range(wide // gr)], axis=1)

    def diag8(f):
        out = f[0:c]
        for b in range(1, grp):
            out = jnp.where(lchunk == b, f[b * c:(b + 1) * c], out)
        return out

    def inverse8(a):
        p = jnp.where(same8, -a, 0.0)
        t = eye8 + p
        p2 = mm(p, bd(p))
        r = mm(jnp.concatenate([t, p2], axis=0), bd(p2))
        t = t + r[0:c]
        t = t + mm(t, bd(r[c:2 * c]))
        for sh in (3, 4, 5):
            off = ((ii >> (sh + 1)) == (jj >> (sh + 1))) & ((ii >> sh) != (jj >> sh))
            l_mat = jnp.where(off, a, 0.0)
            t = t - mm(t, bd(mm(l_mat, bd(t))))
        return t

    def chunk_sums(x):
        hi = x.astype(BF16)
        r1 = x - hi.astype(F32)
        mid = r1.astype(BF16)
        lo = (r1 - mid.astype(F32)).astype(BF16)
        parts = jnp.concatenate([hi, mid, lo], axis=1)
        pre = _dot(tril_ref[...], parts)
        suf = _dot(triu_ref[...], parts)
        fold = lambda r: r[:, 0:128] + r[:, 128:256] + r[:, 256:384]
        return fold(pre), fold(suf)

    def pick(x, lane_id):
        col = jnp.sum(jnp.where(lane_g == lane_id, x, 0.0), axis=-1, keepdims=True)
        return jnp.broadcast_to(col, (gr, 128))

    def chunks(x):
        return [x[b * c:(b + 1) * c] for b in range(grp)]

    def across(xf, xb):
        both = jnp.where(fwd_half, xf, xb)
        return jnp.concatenate(chunks(both), axis=1)

    def interleave_rows(xf, xb):
        return jnp.concatenate([s for pair in zip(chunks(xf), chunks(xb)) for s in pair], axis=0)

    def group_inputs(gi, head):
        r0 = pl.multiple_of(gi * gr, gr)
        q = qs[pl.ds(r0, gr), :]
        k = ks[pl.ds(r0, gr), :]
        v = vs[pl.ds(r0, gr), :]
        beta_all = gate_s[0, pl.ds(r0, gr), :]
        pre_all = gate_s[1, pl.ds(r0, gr), :]
        suf_all = gate_s[2, pl.ds(r0, gr), :]
        g_all = gate_s[3, pl.ds(r0, gr), :]
        lane_f = 2 * GDN_HEADS + head
        lane_b = 3 * GDN_HEADS + head
        beta = (pick(beta_all, head), pick(beta_all, GDN_HEADS + head))
        gc = (pick(pre_all, lane_f), pick(suf_all, lane_b))
        grest = (pick(suf_all - g_all, lane_f), pick(pre_all - g_all, lane_b))
        kb = k.astype(BF16)
        kdup = interleave_rows(kb, kb)
        r = _dot_nt(jnp.concatenate([q.astype(BF16), kb], axis=0), kdup)
        g_j = interleave_rows(gc[0], gc[1]).T[0:c]
        return dict(q=q, k=k, v=v, beta=beta, gc=gc, grest=grest, qk8=diag8(r[0:gr]), kk8=diag8(r[gr:2 * gr]),
                    diff=across(gc[0], gc[1]) - g_j, beta8=across(beta[0], beta[1]))

    def prep(si, carry, head, hh):
        groups = [group_inputs(si * sup + g, head) for g in range(sup)]
        cat = lambda name: jnp.concatenate([g[name] for g in groups], axis=1)
        decay = jnp.where(incl, jnp.exp(jnp.minimum(cat("diff"), 0.0)), 0.0)
        t_all = inverse8(jnp.where(strict, cat("beta8") * cat("kk8") * decay, 0.0))
        qkd_all = cat("qk8") * decay
        for gidx, grp_in in enumerate(groups):
            q, k, v, beta, gc, grest = (grp_in[name] for name in ("q", "k", "v", "beta", "gc", "grest"))
            t8 = t_all[:, gidx * w8:(gidx + 1) * w8]
            qkd8 = qkd_all[:, gidx * w8:(gidx + 1) * w8]
            rhs, qd, kd, eg = [], [], [], []
            for d in range(2):
                e = jnp.exp(gc[d])
                eg.append(e)
                rhs.append(jnp.concatenate([(k * (beta[d] * e)).astype(BF16), (v * beta[d]).astype(BF16)], axis=1))
                qd.append((q * e).astype(BF16))
                kd.append(k * jnp.exp(grest[d]))
            for b in range(grp):
                n = (si * sup + gidx) * grp + b
                rows = slice(b * c, (b + 1) * c)
                tb = t8[:, 2 * c * b:2 * c * (b + 1)]
                lhs = jnp.concatenate([jnp.where(fwd_half_c, tb, 0.0), jnp.where(fwd_half_c, 0.0, tb)], axis=0)
                sol = _dot(lhs.astype(BF16), jnp.concatenate([rhs[0][rows], rhs[1][rows]], axis=0))
                for d in range(2):
                    wu_s[2 * hh + d, n] = sol[d * c:(d + 1) * c].astype(BF16)
                    qd_s[2 * hh + d, n] = qd[d][rows]
                    lo = 2 * c * b + c * d
                    kq_s[2 * hh + d, n] = jnp.concatenate([kd[d][rows].T, qkd8[:, lo:lo + c]], axis=0).astype(BF16)
                    edge = b * c + c - 1 if d == 0 else b * c
                    gl_s[2 * hh + d, n] = jnp.broadcast_to(eg[d][edge:edge + 1, :], (8, 128))
        return carry

    def gates(gi, carry):
        r0 = pl.multiple_of(gi * gr, gr)
        ba = ba_ref[0, pl.ds(r0, gr), :]
        g_all = -jnp.exp(alog_ref[...]) * _softplus(ba + dtb_ref[...])
        pre_all, suf_all = chunk_sums(g_all)
        gate_s[0, pl.ds(r0, gr), :] = jax.nn.sigmoid(ba)
        gate_s[1, pl.ds(r0, gr), :] = pre_all
        gate_s[2, pl.ds(r0, gr), :] = suf_all
        gate_s[3, pl.ds(r0, gr), :] = g_all
        return carry

    lax.fori_loop(0, ng, gates, 0, unroll=3)

    for hh in range(hp):
        head = pl.program_id(1) * hp + hh
        ls = slice(hh * 128, (hh + 1) * 128)
        qs[...] = l2n(short_conv(q_ref[0, :, ls].astype(F32), cwq_ref[:, ls])) * (dk ** -0.5)
        ks[...] = l2n(short_conv(k_ref[0, :, ls].astype(F32), cwk_ref[:, ls]))
        vs[...] = short_conv(v_ref[0, :, ls].astype(F32), cwv_ref[:, ls])
        lax.fori_loop(0, ng // sup, functools.partial(prep, head=head, hh=hh), 0)

    def chunk_of(i, ch):
        return i if ch % 2 == 0 else jnp.where(i < ncc, ncc - 1 - i, nc - 1 - (i - ncc))

    def prepare(i, slot):
        for ch in range(2 * hp):
            n = chunk_of(i, ch)
            r3 = _dot(kq_s[ch, n], wu_s[ch, n])
            q_eff = qd_s[ch, n].astype(F32) - r3[dk:dk + c, 0:dk]
            bq_s[slot, ch] = jnp.concatenate([r3[0:dk, 0:dk], q_eff], axis=0).astype(BF16)
            a_s[slot, ch] = r3[0:dk, dk:]
            c_s[slot, ch] = r3[dk:dk + c, dk:]

    prepare(0, 0)

    def step(i, states):
        slot = i % 2
        new = []
        for ch in range(2 * hp):
            n = chunk_of(i, ch)
            state = states[ch]
            r = _dot(bq_s[slot, ch], state.astype(BF16))
            o_s[ch, pl.ds(pl.multiple_of(n * c, c), c), :] = r[dk:dk + c] + c_s[slot, ch]
            new.append(gl_s[ch, n][0:1, :] * state + a_s[slot, ch] - r[0:dk])
        prepare(jnp.minimum(i + 1, nc - 1), 1 - slot)
        return tuple(new)

    zero = jnp.zeros((dk, GDN_DV), F32)
    lax.fori_loop(0, nc, step, (zero,) * (2 * hp))

    for hh in range(hp):
        ls = slice(hh * 128, (hh + 1) * 128)
        o = _rms(o_s[2 * hh] + o_s[2 * hh + 1], gain_ref[...]) * _silu(z_ref[0, :, ls].astype(F32))
        o_ref[0, :, ls] = o.astype(o_ref.dtype)


def _gdn(qkv, z, ba, conv_w, alog_row, dtb_row, gain, ctx):
    b, s, _ = qkv.shape
    h = GDN_HEADS
    c = GDN_CHUNK
    assert s % (GDN_GROUP * c) == 0
    nc = s // c
    gr = GDN_GROUP * c
    blk = lambda n: jnp.arange(n) // c
    bd_ones = (blk(gr)[:, None] == blk(gr)[None, :]).astype(BF16)
    same = blk(gr)[:, None] == blk(gr)[None, :]
    tri_lo = (same & (jnp.arange(gr)[:, None] >= jnp.arange(gr)[None, :])).astype(BF16)
    tri_up = (same & (jnp.arange(gr)[:, None] <= jnp.arange(gr)[None, :])).astype(BF16)
    hp = GDN_HEADS_PER_STEP
    lanes = hp * 128
    col = lambda off: pl.BlockSpec((1, s, lanes), lambda i, j: (i, 0, off // hp + j))
    cw = lambda off: pl.BlockSpec((SHORT_CONV, lanes), lambda i, j: (0, off // hp + j))
    ch = 2 * hp
    scratch = [pltpu.VMEM((s + 3 * GDN_ZERO_ROWS, 128), F32), pltpu.VMEM((4, s, 128), F32),
               pltpu.VMEM((s, 128), F32), pltpu.VMEM((s, 128), F32), pltpu.VMEM((s, 128), F32),
               pltpu.VMEM((ch, nc, c, GDN_DK + GDN_DV), BF16), pltpu.VMEM((ch, nc, c, GDN_DK), BF16),
               pltpu.VMEM((ch, nc, GDN_DK + c, c), BF16), pltpu.VMEM((ch, nc, 8, 128), F32),
               pltpu.VMEM((2, ch, GDN_DK + c, GDN_DV), BF16), pltpu.VMEM((2, ch, GDN_DK, GDN_DV), F32),
               pltpu.VMEM((2, ch, c, GDN_DV), F32),
               pltpu.VMEM((ch, s, 128), F32)]
    return pl.pallas_call(
        functools.partial(_gdn_kernel, ctx=ctx, seq=s, hp=hp),
        grid=(b, h // hp),
        in_specs=[col(0), col(h), col(2 * h),
                  pl.BlockSpec((1, s, lanes), lambda i, j: (i, 0, j)),
                  pl.BlockSpec((1, s, 128), lambda i, j: (i, 0, 0)),
                  cw(0), cw(h), cw(2 * h),
                  _const_spec((1, 128)), _const_spec((1, 128)), _const_spec((1, 128)),
                  _const_spec(bd_ones.shape), _const_spec(tri_lo.shape), _const_spec(tri_up.shape)],
        out_specs=pl.BlockSpec((1, s, lanes), lambda i, j: (i, 0, j)),
        out_shape=jax.ShapeDtypeStruct((b, s, h * GDN_DV), BF16),
        scratch_shapes=scratch,
        compiler_params=_params(("parallel", "parallel")),
        name="gdn",
    )(qkv, qkv, qkv, z, ba, conv_w, conv_w, conv_w, alog_row, dtb_row, gain.reshape(1, -1), bd_ones, tri_lo, tri_up)


def _rope_tables(n_lat, ctx):
    t = jnp.arange(n_lat)
    row = (t // GRID_W).astype(F32)
    col = (t % GRID_W).astype(F32)
    quarter = MLA_ROPE // 4
    inv_freq = ROPE_BASE ** (-jnp.arange(quarter, dtype=F32) / quarter)
    ang_r = row[:, None] * inv_freq
    ang_c = col[:, None] * inv_freq
    ang = jnp.concatenate([ang_r, ang_r, ang_c, ang_c], axis=-1)
    cos = jnp.concatenate([jnp.ones((ctx, MLA_ROPE), F32), jnp.cos(ang)], axis=0)
    sin = jnp.concatenate([jnp.zeros((ctx, MLA_ROPE), F32), jnp.sin(ang)], axis=0)
    s = ctx + n_lat
    one = jnp.ones((s, MLA_NOPE), F32)
    zn = jnp.zeros((s, MLA_NOPE), F32)
    zp = jnp.zeros((s, MLA_HEAD_PAD - MLA_NOPE - MLA_ROPE), F32)
    cosq = jnp.concatenate([one, cos, zp], axis=1)
    sinq = jnp.concatenate([zn, sin, zp], axis=1)
    zk = jnp.zeros((s, 128 - MLA_ROPE), F32)
    cosk = jnp.concatenate([cos, zk], axis=1)
    sink = jnp.concatenate([sin, zk], axis=1)
    return cosq, sinq, cosk, sink


def _rot_cols(w):
    q = MLA_ROPE // 4
    a, b, c, d = w[..., 0:q], w[..., q:2 * q], w[..., 2 * q:3 * q], w[..., 3 * q:4 * q]
    return jnp.concatenate([-b, a, -d, c], axis=-1)


def _even_weights(w_in, w_uq, w_ukv, w_out):
    d = w_in.shape[0]
    h = GDN_HEADS
    qkv_w = 2 * h * GDN_DK + h * GDN_DV
    vw = h * GDN_DV
    o = 0
    w_qkv = w_in[:, o:o + qkv_w]; o += qkv_w
    w_z = w_in[:, o:o + vw]; o += vw
    w_ba = w_in[:, o:o + 4 * h]; o += 4 * h
    w_cq = w_in[:, o:o + MLA_Q_RANK]; o += MLA_Q_RANK
    w_ckv = w_in[:, o:o + MLA_KV_RANK]; o += MLA_KV_RANK
    w_kr = w_in[:, o:o + MLA_ROPE]
    z64 = jnp.zeros((d, 128 - MLA_ROPE), w_in.dtype)
    w_ba = jnp.concatenate([w_ba, jnp.zeros((d, 128 - 4 * h), w_in.dtype)], axis=1)
    w_mla = jnp.concatenate([w_cq, w_ckv, w_kr, z64, _rot_cols(w_kr), z64], axis=1)
    proj = [w_qkv.astype(BF16), w_z.astype(BF16), w_ba.astype(BF16), w_mla.astype(BF16)]

    rq = MLA_Q_RANK
    wq = w_uq.reshape(rq, MLA_HEADS, MLA_NOPE + MLA_ROPE)
    zpad = jnp.zeros((rq, MLA_HEADS, MLA_HEAD_PAD - MLA_NOPE - MLA_ROPE), w_uq.dtype)
    znope = jnp.zeros((rq, MLA_HEADS, MLA_NOPE), w_uq.dtype)
    wqa = jnp.concatenate([wq, zpad], axis=-1).reshape(rq, -1)
    wqb = jnp.concatenate([znope, _rot_cols(wq[..., MLA_NOPE:]), zpad], axis=-1).reshape(rq, -1)
    rk = MLA_KV_RANK
    wkv = w_ukv.reshape(rk, MLA_HEADS, MLA_NOPE + MLA_V)
    wk = jnp.concatenate([wkv[..., :MLA_NOPE], jnp.zeros((rk, MLA_HEADS, MLA_HEAD_PAD - MLA_NOPE), w_ukv.dtype)],
                         axis=-1).reshape(rk, -1)
    wv = wkv[..., MLA_NOPE:].reshape(rk, -1)
    eye = jnp.eye(MLA_ROPE, dtype=F32)
    blk = jnp.concatenate([jnp.zeros((MLA_ROPE, MLA_NOPE), F32), eye,
                           jnp.zeros((MLA_ROPE, MLA_HEAD_PAD - MLA_NOPE - MLA_ROPE), F32)], axis=1)
    pk = jnp.concatenate([jnp.tile(blk, (1, MLA_HEADS)),
                          jnp.zeros((128 - MLA_ROPE, MLA_HEADS * MLA_HEAD_PAD), F32)], axis=0)
    up = [wqa.astype(BF16), wqb.astype(BF16), wk.astype(BF16), wv.astype(BF16), pk.astype(BF16)]
    out = [w_out[:vw].astype(BF16), w_out[vw:].astype(BF16)]
    return proj, up, out


def _gate_rows(a_log, dt_bias):
    n = 2 * GDN_HEADS
    z = jnp.zeros((n,), F32)
    tail = jnp.zeros((128 - 2 * n,), F32)
    alog_row = jnp.concatenate([z, a_log.astype(F32).reshape(n), tail]).reshape(1, 128)
    dtb_row = jnp.concatenate([z, dt_bias.astype(F32).reshape(n), tail]).reshape(1, 128)
    return alog_row, dtb_row


def kernel(x, c, ctx, c_ctx, ada_w, ada_b, norm_mix_g, norm_ffn_g, ev_w_in, ev_conv_qkv, ev_a_log, ev_dt_bias,
           ev_gdn_norm, ev_q_norm, ev_kv_norm, ev_w_uq, ev_w_ukv, ev_w_out, od_w_qkv, od_rpb, od_w_out,
           ffn_w_in, ffn_conv, ffn_conv_b, ffn_w_out, final_g):
    b, n_lat, d = x.shape
    n_ctx = ctx.shape[1]
    depth = ada_w.shape[0]
    tm = n_ctx
    assert n_lat % tm == 0 and n_lat % GRID_W == 0 and tm % GDN_CHUNK == 0
    rows = n_lat // GRID_W
    assert rows >= NA_KH

    r_pad = -(b + 1) % 8
    cond = jnp.concatenate([c, c_ctx[None, :], jnp.zeros((r_pad, d), c.dtype)], axis=0)
    mods = _ada_table(cond, ada_w, ada_b).reshape(depth, b + 1 + r_pad, 6, d)
    tabs = _rope_tables(n_lat, n_ctx)
    d_ff = ffn_w_out.shape[1]

    xa = jnp.concatenate([ctx, x], axis=1)
    for i in range(depth):
        j = i // 2
        m = mods[i]
        if i % 2 == 0:
            proj_w, up_w, out_w = _even_weights(ev_w_in[j], ev_w_uq[j], ev_w_ukv[j], ev_w_out[j])
            qkv, z, ba, p_mla = _norm_mod_matmul(xa, m, norm_mix_g[i], proj_w, [BF16, BF16, F32, F32], 0, tm)
            alog_row, dtb_row = _gate_rows(ev_a_log[j], ev_dt_bias[j])
            y_a = _gdn(qkv, z, ba, ev_conv_qkv[j], alog_row, dtb_row, ev_gdn_norm[j], n_ctx)
            q, k, v = _mla_up(p_mla, tabs, ev_q_norm[j], ev_kv_norm[j], *up_w, tm)
            y_b = _mla_attention(q, k, v, n_ctx, tm)
            ys, w_outs = [y_a, y_b], out_w
        else:
            w = od_w_qkv[j]
            nw = w.shape[1] // 3
            ws = [(w[:, :nw] * (NA_DH ** -0.5)).astype(BF16), w[:, nw:2 * nw].astype(BF16),
                  w[:, 2 * nw:].astype(BF16)]
            q, k, v = _norm_mod_matmul(xa, m, norm_mix_g[i], ws, [BF16, BF16, BF16], 0, tm)
            ys, w_outs = [_na_attention(q, k, v, _na_bias_table(od_rpb[j], rows), n_ctx)], [od_w_out[j].astype(BF16)]
        wi = ffn_w_in[i]
        xa = _mix_ffn_residual(xa, m, ys, w_outs, norm_ffn_g[i], wi[:, :d_ff].astype(BF16),
                               wi[:, d_ff:].astype(BF16), ffn_conv[i], ffn_conv_b[i], ffn_w_out[i].astype(BF16), tm,
                               final_g=final_g if i == depth - 1 else None)
    return xa
```

```python
import functools
import math

import jax
import jax.numpy as jnp
from jax import lax
from jax.experimental import pallas as pl
from jax.experimental.pallas import tpu as pltpu

EPS = 1e-6
ROPE_BASE = 10000.0
GRID_W = 64

GDN_HEADS = 4
GDN_DK = 128
GDN_DV = 128
GDN_CHUNK = 64
GDN_GROUP = 4
GDN_HEADS_PER_STEP = 2
GDN_ZERO_ROWS = 8
SHORT_CONV = 5

MLA_HEADS = 4
MLA_NOPE = 128
MLA_ROPE = 64
MLA_V = 128
MLA_Q_RANK = 256
MLA_KV_RANK = 256
MLA_HEAD_PAD = 256

NA_HEADS = 16
NA_DH = 64
NA_KH = 8
NA_KW = 16
NA_PATTERNS = 8

FFN_CONV = 3
HALO = 16

VMEM_LIMIT = 56 * 1024 * 1024
NEG_BIG = -1e30

BF16 = jnp.bfloat16
F32 = jnp.float32


def _dot(a, b):
    return jnp.dot(a, b, preferred_element_type=F32)


def _dot_nt(a, b):
    return lax.dot_general(a, b, (((1,), (1,)), ((), ())), preferred_element_type=F32)


def _silu(x):
    return x * jax.nn.sigmoid(x)


def _softplus(x):
    return jnp.maximum(x, 0.0) + jnp.log(1.0 + jnp.exp(-jnp.abs(x)))


def _rms(x, g):
    return x * lax.rsqrt(jnp.mean(x * x, axis=-1, keepdims=True) + EPS) * g


def _params(sem):
    return pltpu.CompilerParams(dimension_semantics=sem, vmem_limit_bytes=VMEM_LIMIT)


def _const_spec(shape):
    nd = len(shape)
    return pl.BlockSpec(shape, lambda *_: (0,) * nd)


def _ada_kernel(c_ref, w_ref, b_ref, o_ref):
    s = _silu(c_ref[...]).astype(BF16)
    o_ref[0] = _dot(s, w_ref[0]) + b_ref[0]


def _ada_table(cond, ada_w, ada_b):
    depth, d, n = ada_w.shape
    r = cond.shape[0]
    tn = 1536
    return pl.pallas_call(
        _ada_kernel,
        grid=(depth, n // tn),
        in_specs=[pl.BlockSpec((r, d), lambda i, j: (0, 0)),
                  pl.BlockSpec((1, d, tn), lambda i, j: (i, 0, j)),
                  pl.BlockSpec((1, 1, tn), lambda i, j: (i, 0, j))],
        out_specs=pl.BlockSpec((1, r, tn), lambda i, j: (i, 0, j)),
        out_shape=jax.ShapeDtypeStruct((depth, r, n), F32),
        compiler_params=_params(("parallel", "parallel")),
        name="ada_table",
    )(cond, ada_w.astype(BF16), ada_b.reshape(depth, 1, n))


def _nmm_kernel(x_ref, mod_ref, g_ref, *rest, n_out, shift_idx):
    w_refs, o_refs = rest[:n_out], rest[n_out:]
    shift = mod_ref[0, shift_idx:shift_idx + 1, :]
    scale = mod_ref[0, shift_idx + 1:shift_idx + 2, :]
    h = _rms(x_ref[0], g_ref[...]) * (1.0 + scale) + shift
    hb = h.astype(BF16)
    for w_ref, o_ref in zip(w_refs, o_refs):
        o_ref[0] = _dot(hb, w_ref[...]).astype(o_ref.dtype)


def _norm_mod_matmul(xa, mods, g, ws, out_dtypes, shift_idx, tm):
    b, s, d = xa.shape
    n_out = len(ws)
    in_specs = [pl.BlockSpec((1, tm, d), lambda i, t: (i, t, 0)),
                pl.BlockSpec((1, 6, d), lambda i, t: (jnp.where(t == 0, b, i), 0, 0)),
                _const_spec((1, d))]
    in_specs += [_const_spec(w.shape) for w in ws]
    out_specs = [pl.BlockSpec((1, tm, w.shape[1]), lambda i, t: (i, t, 0)) for w in ws]
    out_shape = [jax.ShapeDtypeStruct((b, s, w.shape[1]), dt) for w, dt in zip(ws, out_dtypes)]
    return pl.pallas_call(
        functools.partial(_nmm_kernel, n_out=n_out, shift_idx=shift_idx),
        grid=(b, s // tm),
        in_specs=in_specs, out_specs=out_specs, out_shape=out_shape,
        compiler_params=_params(("parallel", "parallel")),
        name="norm_mod_matmul",
    )(xa, mods, g.reshape(1, d), *ws)


def _mix_ffn_kernel(x_ref, xp_ref, xn_ref, *rest, n_in, tm, n_tiles, t_off, final):
    y_refs = rest[:3 * n_in]
    w_refs = rest[3 * n_in:4 * n_in]
    mod_ref, g_ref, wu_ref, wg_ref, cw_ref, cb_ref, wo_ref, fg_ref, o_ref = rest[4 * n_in:]
    t = pl.program_id(1) + t_off
    ext = lambda main, prev, nxt: jnp.concatenate([prev[0], main[0], nxt[0]], axis=0)
    proj = _dot(ext(*y_refs[0:3]), w_refs[0][...])
    for j in range(1, n_in):
        proj = proj + _dot(ext(*y_refs[3 * j:3 * j + 3]), w_refs[j][...])
    x_ext = ext(x_ref, xp_ref, xn_ref) + mod_ref[0, 2:3, :] * proj
    x = x_ext[HALO:HALO + tm]
    shift = mod_ref[0, 3:4, :]
    scale = mod_ref[0, 4:5, :]
    gate = mod_ref[0, 5:6, :]
    h_ext = (_rms(x_ext, g_ref[...]) * (1.0 + scale) + shift).astype(BF16)
    gt = _dot(h_ext, wg_ref[...])
    u = _dot(h_ext[HALO:HALO + tm], wu_ref[...])
    prev_ok = (t >= 2).astype(F32)
    next_ok = jnp.logical_and(t >= 1, t < n_tiles - 1).astype(F32)
    row = lax.broadcasted_iota(jnp.int32, (tm + 2 * HALO, 1), 0)
    keep = jnp.where(row < HALO, prev_ok, jnp.where(row >= tm + HALO, next_ok, 1.0))
    gt = gt * keep
    g_prev = pltpu.roll(gt, 1, axis=0)[HALO:HALO + tm]
    g_next = pltpu.roll(gt, tm + 2 * HALO - 1, axis=0)[HALO:HALO + tm]
    g_mid = gt[HALO:HALO + tm]
    conv = g_prev * cw_ref[0:1, :] + g_mid * cw_ref[1:2, :] + g_next * cw_ref[2:3, :] + cb_ref[...]
    act = (_silu(conv) * u).astype(BF16)
    y = x + gate * _dot(act, wo_ref[...])
    o_ref[0] = _rms(y, fg_ref[...]) if final else y


def _mix_ffn_residual(xa, mods, ys, w_outs, g, wu, wg, conv_w, conv_b, wo, tm, final_g=None):
    b, s, d = xa.shape
    f = wu.shape[1]
    n_tiles = s // tm
    hb = tm // HALO
    last = s // HALO - 1
    final = final_g is not None
    t_off = 1 if final else 0
    fg = final_g if final else g
    mod_row = (lambda i, t: (i, 0, 0)) if final else (lambda i, t: (jnp.where(t == 0, b, i), 0, 0))
    main = lambda w: pl.BlockSpec((1, tm, w), lambda i, t: (i, t + t_off, 0))
    prev = lambda w: pl.BlockSpec((1, HALO, w), lambda i, t: (i, jnp.maximum((t + t_off) * hb - 1, 0), 0))
    nxt = lambda w: pl.BlockSpec((1, HALO, w), lambda i, t: (i, jnp.minimum((t + t_off + 1) * hb, last), 0))
    in_specs = [main(d), prev(d), nxt(d)]
    args = [xa, xa, xa]
    for y in ys:
        in_specs += [main(y.shape[2]), prev(y.shape[2]), nxt(y.shape[2])]
        args += [y, y, y]
    in_specs += [_const_spec(w.shape) for w in w_outs]
    in_specs += [pl.BlockSpec((1, 6, d), mod_row), _const_spec((1, d)), _const_spec((d, f)), _const_spec((d, f)),
                 _const_spec((FFN_CONV, f)), _const_spec((1, f)), _const_spec((f, d)), _const_spec((1, d))]
    args += list(w_outs) + [mods, g.reshape(1, d), wu, wg, conv_w, conv_b.reshape(1, f), wo, fg.reshape(1, d)]
    return pl.pallas_call(
        functools.partial(_mix_ffn_kernel, n_in=len(ys), tm=tm, n_tiles=n_tiles, t_off=t_off, final=final),
        grid=(b, n_tiles - t_off),
        in_specs=in_specs,
        out_specs=pl.BlockSpec((1, tm, d), lambda i, t: (i, t, 0)),
        out_shape=jax.ShapeDtypeStruct((b, s - t_off * tm, d), F32),
        compiler_params=_params(("parallel", "parallel")),
        name="mix_ffn",
    )(*args)


def _even_proj_kernel(x_ref, mod_ref, g_ref, wqkv_ref, wz_ref, wba_ref, wmla_ref,
                      cq_ref, sq_ref, ck_ref, sk_ref, qn_ref, kvn_ref, wqa_ref, wqb_ref, wk_ref, wv_ref, pk_ref,
                      qkv_ref, z_ref, ba_ref, q_ref, k_ref, v_ref, *, scale):
    h = _rms(x_ref[0], g_ref[...]) * (1.0 + mod_ref[0, 1:2, :]) + mod_ref[0, 0:1, :]
    hb = h.astype(BF16)
    qkv_ref[0] = _dot(hb, wqkv_ref[...]).astype(qkv_ref.dtype)
    z_ref[0] = _dot(hb, wz_ref[...]).astype(z_ref.dtype)
    ba_ref[0] = _dot(hb, wba_ref[...])
    p = _dot(hb, wmla_ref[...])
    r = MLA_Q_RANK
    cq = _rms(p[:, 0:r], qn_ref[...]).astype(BF16)
    ckv = _rms(p[:, r:2 * r], kvn_ref[...]).astype(BF16)
    k_rope = (p[:, 2 * r:2 * r + 128] * ck_ref[...] + p[:, 2 * r + 128:2 * r + 256] * sk_ref[...]).astype(BF16)
    k_ref[0] = (_dot(ckv, wk_ref[...]) + _dot(k_rope, pk_ref[...])).astype(BF16)
    v_ref[0] = _dot(ckv, wv_ref[...]).astype(BF16)
    qa = _dot(cq, wqa_ref[...])
    qb = _dot(cq, wqb_ref[...])
    cq_t = cq_ref[...] * scale
    sq_t = sq_ref[...] * scale
    for hd in range(MLA_HEADS):
        sl = slice(hd * MLA_HEAD_PAD, (hd + 1) * MLA_HEAD_PAD)
        q_ref[0, :, sl] = (qa[:, sl] * cq_t + qb[:, sl] * sq_t).astype(BF16)


def _even_projections(xa, mods, g, proj_w, tabs, q_norm, kv_norm, up_w, tm):
    b, s, d = xa.shape
    cosq, sinq, cosk, sink = tabs
    scale = (MLA_NOPE + MLA_ROPE) ** -0.5
    hw = MLA_HEADS * MLA_HEAD_PAD
    widths = [proj_w[0].shape[1], proj_w[1].shape[1], proj_w[2].shape[1], hw, hw, MLA_HEADS * MLA_V]
    dtypes = [BF16, BF16, F32, BF16, BF16, BF16]
    row = lambda w: pl.BlockSpec((tm, w), lambda i, t: (t, 0))
    in_specs = [pl.BlockSpec((1, tm, d), lambda i, t: (i, t, 0)),
                pl.BlockSpec((1, 6, d), lambda i, t: (jnp.where(t == 0, b, i), 0, 0)),
                _const_spec((1, d))]
    in_specs += [_const_spec(w.shape) for w in proj_w]
    in_specs += [row(MLA_HEAD_PAD), row(MLA_HEAD_PAD), row(128), row(128),
                 _const_spec((1, MLA_Q_RANK)), _const_spec((1, MLA_KV_RANK))]
    in_specs += [_const_spec(w.shape) for w in up_w]
    return pl.pallas_call(
        functools.partial(_even_proj_kernel, scale=scale),
        grid=(b, s // tm),
        in_specs=in_specs,
        out_specs=[pl.BlockSpec((1, tm, w), lambda i, t: (i, t, 0)) for w in widths],
        out_shape=[jax.ShapeDtypeStruct((b, s, w), dt) for w, dt in zip(widths, dtypes)],
        compiler_params=_params(("parallel", "parallel")),
        name="even_projections",
    )(xa, mods, g.reshape(1, d), *proj_w, cosq, sinq, cosk, sink, q_norm.reshape(1, -1), kv_norm.reshape(1, -1),
      *up_w)


def _softmax_pv(s, v):
    m = jnp.max(s, axis=-1, keepdims=True)
    p = jnp.exp(s - m)
    l = jnp.sum(p, axis=-1, keepdims=True)
    return _dot(p.astype(BF16), v) / l


def _mla_attn_kernel(q_ref, k_ref, v_ref, o_ref, *, ctx):
    t = pl.program_id(2)
    q = q_ref[0]

    @pl.when(t == 0)
    def _():
        o_ref[0] = _softmax_pv(_dot_nt(q, k_ref[0, 0:ctx, :]), v_ref[0, 0:ctx, :]).astype(o_ref.dtype)

    @pl.when(t > 0)
    def _():
        o_ref[0] = _softmax_pv(_dot_nt(q, k_ref[0]), v_ref[0]).astype(o_ref.dtype)


def _mla_attention(q, k, v, ctx, tq):
    b, s, _ = q.shape
    return pl.pallas_call(
        functools.partial(_mla_attn_kernel, ctx=ctx),
        grid=(b, MLA_HEADS, s // tq),
        in_specs=[pl.BlockSpec((1, tq, MLA_HEAD_PAD), lambda i, h, t: (i, t, h)),
                  pl.BlockSpec((1, s, MLA_HEAD_PAD), lambda i, h, t: (i, 0, h)),
                  pl.BlockSpec((1, s, MLA_V), lambda i, h, t: (i, 0, h))],
        out_specs=pl.BlockSpec((1, tq, MLA_V), lambda i, h, t: (i, t, h)),
        out_shape=jax.ShapeDtypeStruct((b, s, MLA_HEADS * MLA_V), BF16),
        compiler_params=_params(("parallel", "parallel", "parallel")),
        name="mla_attention",
    )(q, k, v)


def _na_kernel(q_ref, k_ref, v_ref, bias_ref, o_ref, s0, s1, p0, p1, l0, l1, *, ctx, rows):
    w = GRID_W
    win = NA_KH * w
    s_scr, p_scr, l_scr = (s0, s1), (p0, p1), (l0, l1)
    lane = lax.broadcasted_iota(jnp.int32, (w, 2 * NA_DH), 1)
    first = lane < NA_DH
    kc = k_ref[0, 0:ctx, :]
    vc = v_ref[0, 0:ctx, :]

    def split_heads(qr, first_mask):
        zero = jnp.zeros_like(qr)
        return jnp.concatenate([jnp.where(first_mask, qr, zero), jnp.where(first_mask, zero, qr)], axis=0)

    def merge_heads(o2):
        n = o2.shape[0] // 2
        lane_n = lax.broadcasted_iota(jnp.int32, (n, 2 * NA_DH), 1)
        return jnp.where(lane_n < NA_DH, o2[:n], o2[n:])

    lane_c = lax.broadcasted_iota(jnp.int32, (ctx, 2 * NA_DH), 1)
    qc2 = split_heads(q_ref[0, 0:ctx, :], lane_c < NA_DH)
    o_ref[0, 0:ctx, :] = merge_heads(_softmax_pv(_dot_nt(qc2, kc), vc)).astype(o_ref.dtype)

    def window_start(r):
        return pl.multiple_of(ctx + jnp.clip(r - NA_KH // 2, 0, rows - NA_KH) * w, w)

    def scores(r, slot):
        pat = jnp.where(r < NA_KH // 2, r,
                        jnp.where(r <= rows - NA_KH // 2, NA_KH // 2, r - (rows - NA_KH)))
        q2 = split_heads(q_ref[0, pl.ds(pl.multiple_of(ctx + r * w, w), w), :], first)
        kw = k_ref[0, pl.ds(window_start(r), win), :]
        s_scr[slot][:, 0:win] = _dot_nt(q2, kw) + bias_ref[0, pat]
        s_scr[slot][:, win:win + ctx] = _dot_nt(q2, kc)

    def softmax(slot):
        s = s_scr[slot][...]
        m = jnp.max(s, axis=-1, keepdims=True)
        p = jnp.exp(s - m)
        l_scr[slot][...] = jnp.broadcast_to(jnp.sum(p, axis=-1, keepdims=True), (2 * w, 2 * NA_DH))
        p_scr[slot][...] = p.astype(BF16)

    def values(r, slot):
        vw = v_ref[0, pl.ds(window_start(r), win), :]
        o2 = (_dot(p_scr[slot][:, 0:win], vw) + _dot(p_scr[slot][:, win:win + ctx], vc)) / l_scr[slot][...]
        o_ref[0, pl.ds(pl.multiple_of(ctx + r * w, w), w), :] = merge_heads(o2).astype(o_ref.dtype)

    scores(0, 0)
    scores(1, 1)
    softmax(0)

    def two_rows(u, carry):
        t = 2 * u
        scores(t, 0)
        softmax(1)
        values(t - 2, 0)
        scores(t + 1, 1)
        softmax(0)
        values(t - 1, 1)
        return carry

    lax.fori_loop(1, rows // 2, two_rows, 0)
    softmax(1)
    values(rows - 2, 0)
    values(rows - 1, 1)


def _na_attention(q, k, v, bias, ctx):
    b, s, width = q.shape
    rows = (s - ctx) // GRID_W
    assert rows % 2 == 0
    pairs = width // (2 * NA_DH)
    keys = NA_KH * GRID_W + ctx
    blk = pl.BlockSpec((1, s, 2 * NA_DH), lambda h, i: (i, 0, h))
    scratch = ([pltpu.VMEM((2 * GRID_W, keys), F32)] * 2 + [pltpu.VMEM((2 * GRID_W, keys), BF16)] * 2
               + [pltpu.VMEM((2 * GRID_W, 2 * NA_DH), F32)] * 2)
    return pl.pallas_call(
        functools.partial(_na_kernel, ctx=ctx, rows=rows),
        grid=(pairs, b),
        in_specs=[blk, blk, blk,
                  pl.BlockSpec((1, NA_PATTERNS, 2 * GRID_W, NA_KH * GRID_W), lambda h, i: (h, 0, 0, 0))],
        out_specs=blk,
        out_shape=jax.ShapeDtypeStruct((b, s, width), BF16),
        scratch_shapes=scratch,
        compiler_params=_params(("parallel", "parallel")),
        name="na_attention",
    )(q, k, v, bias)


def _na_bias_kernel(rpb_ref, o_ref, *, rows):
    w, kh, kw = GRID_W, NA_KH, NA_KW
    pair = pl.program_id(0)
    qi = lax.broadcasted_iota(jnp.int32, (w, w), 0)
    ki = lax.broadcasted_iota(jnp.int32, (w, w), 1)
    col_idx = jnp.clip(ki - qi, -(kw - 1), kw - 1) + kw - 1
    c_start = jnp.clip(qi - kw // 2, 0, w - kw)
    masked = jnp.where((ki >= c_start) & (ki < c_start + kw), 0.0, NEG_BIG)
    rep = (0, 1, 2, 3, 4, rows - 3, rows - 2, rows - 1)
    for hh in range(2):
        head = 2 * pair + hh
        tiles = []
        for dr in range(2 * kh - 1):
            t = masked
            for dc in range(2 * kw - 1):
                t = jnp.where(col_idx == dc, masked + rpb_ref[head, dr, dc], t)
            tiles.append(t)
        for p, r in enumerate(rep):
            r_start = min(max(r - kh // 2, 0), rows - kh)
            for j in range(kh):
                o_ref[0, p, hh * w:(hh + 1) * w, j * w:(j + 1) * w] = tiles[r_start + j - r + kh - 1]


def _na_bias_table(rpb, rows):
    h = rpb.shape[0]
    shape = (h // 2, NA_PATTERNS, 2 * GRID_W, NA_KH * GRID_W)
    return pl.pallas_call(
        functools.partial(_na_bias_kernel, rows=rows),
        grid=(h // 2,),
        in_specs=[pl.BlockSpec(memory_space=pltpu.SMEM)],
        out_specs=pl.BlockSpec((1,) + shape[1:], lambda i: (i, 0, 0, 0)),
        out_shape=jax.ShapeDtypeStruct(shape, F32),
        compiler_params=_params(("parallel",)),
        name="na_bias_table",
    )(rpb.astype(F32))


def _gdn_kernel(q_ref, k_ref, v_ref, z_ref, ba_ref, cwq_ref, cwk_ref, cwv_ref, alog_ref, dtb_ref, gain_ref,
                bd_ref, tril_ref, triu_ref, o_ref, xpad, gate_s, qs, ks, vs, wu_s, qd_s, kq_s, gl_s, bq_s, a_s, c_s, o_s,
                *, ctx, seq, hp):
    c = GDN_CHUNK
    dk = GDN_DK
    grp = GDN_GROUP
    gr = grp * c
    nc = seq // c
    ncc = ctx // c
    ng = seq // gr

    pad = SHORT_CONV // 2
    zp = GDN_ZERO_ROWS

    def short_conv(x, cw):
        zeros = jnp.zeros((zp, 128), F32)
        xpad[0:zp] = zeros
        xpad[zp:zp + ctx] = x[0:ctx]
        xpad[zp + ctx:2 * zp + ctx] = zeros
        xpad[2 * zp + ctx:2 * zp + seq] = x[ctx:seq]
        xpad[2 * zp + seq:3 * zp + seq] = zeros

        def segment(base, n):
            acc = xpad[base - pad:base - pad + n] * cw[0:1, :]
            for j in range(1, SHORT_CONV):
                acc = acc + xpad[base - pad + j:base - pad + j + n] * cw[j:j + 1, :]
            return acc

        return _silu(jnp.concatenate([segment(zp, ctx), segment(2 * zp + ctx, seq - ctx)], axis=0))

    def l2n(x):
        return x * lax.rsqrt(jnp.sum(x * x, axis=-1, keepdims=True) + EPS)

    w8 = 2 * gr
    sup = max(m for m in (1, 2, 3) if ng % m == 0)
    wide = sup * w8
    ii = lax.broadcasted_iota(jnp.int32, (c, wide), 0)
    lane8 = lax.broadcasted_iota(jnp.int32, (c, wide), 1)
    jj = lane8 & (c - 1)
    fwd8 = ((lane8 >> 6) & 1) == 0
    lchunk = lax.broadcasted_iota(jnp.int32, (c, w8), 1) >> 7
    incl = (fwd8 & (ii >= jj)) | (jnp.logical_not(fwd8) & (ii <= jj))
    strict = incl & (ii != jj)
    eye8 = (ii == jj).astype(F32)
    same8 = (ii >> 3) == (jj >> 3)
    lane_g = lax.broadcasted_iota(jnp.int32, (gr, 128), 1)
    fwd_half = lane_g < c
    fwd_half_c = lax.broadcasted_iota(jnp.int32, (c, 128), 1) < c

    def bd(y):
        yb = y.astype(BF16)
        return [jnp.concatenate([yb[:, h * gr:(h + 1) * gr]] * grp, axis=0) * bd_ref[...]
                for h in range(wide // gr)]

    def mm(x, ybd):
        xb = x.astype(BF16)
        return jnp.concatenate([_dot(xb[:, h * gr:(h + 1) * gr], ybd[h]) for h in range(wide // gr)], axis=1)

    def diag8(f):
        out = f[0:c]
        for b in range(1, grp):
            out = jnp.where(lchunk == b, f[b * c:(b + 1) * c], out)
        return out

    def inverse8(a):
        p = jnp.where(same8, -a, 0.0)
        t = eye8 + p
        p2 = mm(p, bd(p))
        r = mm(jnp.concatenate([t, p2], axis=0), bd(p2))
        t = t + r[0:c]
        t = t + mm(t, bd(r[c:2 * c]))
        for sh in (3, 4, 5):
            off = ((ii >> (sh + 1)) == (jj >> (sh + 1))) & ((ii >> sh) != (jj >> sh))
            l_mat = jnp.where(off, a, 0.0)
            t = t - mm(t, bd(mm(l_mat, bd(t))))
        return t

    def chunk_sums(x):
        hi = x.astype(BF16)
        r1 = x - hi.astype(F32)
        mid = r1.astype(BF16)
        lo = (r1 - mid.astype(F32)).astype(BF16)
        parts = jnp.concatenate([hi, mid, lo], axis=1)
        pre = _dot(tril_ref[...], parts)
        suf = _dot(triu_ref[...], parts)
        fold = lambda r: r[:, 0:128] + r[:, 128:256] + r[:, 256:384]
        return fold(pre), fold(suf)

    def pick(x, lane_id):
        col = jnp.sum(jnp.where(lane_g == lane_id, x, 0.0), axis=-1, keepdims=True)
        return jnp.broadcast_to(col, (gr, 128))

    def chunks(x):
        return [x[b * c:(b + 1) * c] for b in range(grp)]

    def across(xf, xb):
        both = jnp.where(fwd_half, xf, xb)
        return jnp.concatenate(chunks(both), axis=1)

    def interleave_rows(xf, xb):
        return jnp.concatenate([s for pair in zip(chunks(xf), chunks(xb)) for s in pair], axis=0)

    def group_inputs(gi, head):
        r0 = pl.multiple_of(gi * gr, gr)
        q = qs[pl.ds(r0, gr), :]
        k = ks[pl.ds(r0, gr), :]
        v = vs[pl.ds(r0, gr), :]
        beta_all = gate_s[0, pl.ds(r0, gr), :]
        pre_all = gate_s[1, pl.ds(r0, gr), :]
        suf_all = gate_s[2, pl.ds(r0, gr), :]
        g_all = gate_s[3, pl.ds(r0, gr), :]
        lane_f = 2 * GDN_HEADS + head
        lane_b = 3 * GDN_HEADS + head
        beta = (pick(beta_all, head), pick(beta_all, GDN_HEADS + head))
        gc = (pick(pre_all, lane_f), pick(suf_all, lane_b))
        grest = (pick(suf_all - g_all, lane_f), pick(pre_all - g_all, lane_b))
        kb = k.astype(BF16)
        kdup = interleave_rows(kb, kb)
        r = _dot_nt(jnp.concatenate([q.astype(BF16), kb], axis=0), kdup)
        g_j = interleave_rows(gc[0], gc[1]).T[0:c]
        return dict(q=q, k=k, v=v, beta=beta, gc=gc, grest=grest, qk8=diag8(r[0:gr]), kk8=diag8(r[gr:2 * gr]),
                    diff=across(gc[0], gc[1]) - g_j, beta8=across(beta[0], beta[1]))

    def prep(si, carry, head, hh):
        groups = [group_inputs(si * sup + g, head) for g in range(sup)]
        cat = lambda name: jnp.concatenate([g[name] for g in groups], axis=1)
        decay = jnp.where(incl, jnp.exp(jnp.minimum(cat("diff"), 0.0)), 0.0)
        t_all = inverse8(jnp.where(strict, cat("beta8") * cat("kk8") * decay, 0.0))
        qkd_all = cat("qk8") * decay
        for gidx, grp_in in enumerate(groups):
            q, k, v, beta, gc, grest = (grp_in[name] for name in ("q", "k", "v", "beta", "gc", "grest"))
            t8 = t_all[:, gidx * w8:(gidx + 1) * w8]
            qkd8 = qkd_all[:, gidx * w8:(gidx + 1) * w8]
            rhs, qd, kd, eg = [], [], [], []
            for d in range(2):
                e = jnp.exp(gc[d])
                eg.append(e)
                rhs.append(jnp.concatenate([(k * (beta[d] * e)).astype(BF16), (v * beta[d]).astype(BF16)], axis=1))
                qd.append((q * e).astype(BF16))
                kd.append(k * jnp.exp(grest[d]))
            for b in range(grp):
                n = (si * sup + gidx) * grp + b
                rows = slice(b * c, (b + 1) * c)
                tb = t8[:, 2 * c * b:2 * c * (b + 1)]
                lhs = jnp.concatenate([jnp.where(fwd_half_c, tb, 0.0), jnp.where(fwd_half_c, 0.0, tb)], axis=0)
                sol = _dot(lhs.astype(BF16), jnp.concatenate([rhs[0][rows], rhs[1][rows]], axis=0))
                for d in range(2):
                    wu_s[2 * hh + d, n] = sol[d * c:(d + 1) * c].astype(BF16)
                    qd_s[2 * hh + d, n] = qd[d][rows]
                    lo = 2 * c * b + c * d
                    kq_s[2 * hh + d, n] = jnp.concatenate([kd[d][rows].T, qkd8[:, lo:lo + c]], axis=0).astype(BF16)
                    edge = b * c + c - 1 if d == 0 else b * c
                    gl_s[2 * hh + d, n] = jnp.broadcast_to(eg[d][edge:edge + 1, :], (8, 128))
        return carry

    def gates(gi, carry):
        r0 = pl.multiple_of(gi * gr, gr)
        ba = ba_ref[0, pl.ds(r0, gr), :]
        g_all = -jnp.exp(alog_ref[...]) * _softplus(ba + dtb_ref[...])
        pre_all, suf_all = chunk_sums(g_all)
        gate_s[0, pl.ds(r0, gr), :] = jax.nn.sigmoid(ba)
        gate_s[1, pl.ds(r0, gr), :] = pre_all
        gate_s[2, pl.ds(r0, gr), :] = suf_all
        gate_s[3, pl.ds(r0, gr), :] = g_all
        return carry

    lax.fori_loop(0, ng, gates, 0, unroll=3)

    for hh in range(hp):
        head = pl.program_id(1) * hp + hh
        ls = slice(hh * 128, (hh + 1) * 128)
        qs[...] = l2n(short_conv(q_ref[0, :, ls].astype(F32), cwq_ref[:, ls])) * (dk ** -0.5)
        ks[...] = l2n(short_conv(k_ref[0, :, ls].astype(F32), cwk_ref[:, ls]))
        vs[...] = short_conv(v_ref[0, :, ls].astype(F32), cwv_ref[:, ls])
        lax.fori_loop(0, ng // sup, functools.partial(prep, head=head, hh=hh), 0)

    def chunk_of(i, ch):
        return i if ch % 2 == 0 else jnp.where(i < ncc, ncc - 1 - i, nc - 1 - (i - ncc))

    def prepare(i, slot):
        for ch in range(2 * hp):
            n = chunk_of(i, ch)
            r3 = _dot(kq_s[ch, n], wu_s[ch, n])
            q_eff = qd_s[ch, n].astype(F32) - r3[dk:dk + c, 0:dk]
            bq_s[slot, ch] = jnp.concatenate([r3[0:dk, 0:dk], q_eff], axis=0).astype(BF16)
            a_s[slot, ch] = r3[0:dk, dk:]
            c_s[slot, ch] = r3[dk:dk + c, dk:]

    prepare(0, 0)

    def step(i, states):
        slot = i % 2
        new = []
        for ch in range(2 * hp):
            n = chunk_of(i, ch)
            state = states[ch]
            r = _dot(bq_s[slot, ch], state.astype(BF16))
            o_s[ch, pl.ds(pl.multiple_of(n * c, c), c), :] = r[dk:dk + c] + c_s[slot, ch]
            new.append(gl_s[ch, n][0:1, :] * state + a_s[slot, ch] - r[0:dk])
        prepare(jnp.minimum(i + 1, nc - 1), 1 - slot)
        return tuple(new)

    zero = jnp.zeros((dk, GDN_DV), F32)
    lax.fori_loop(0, nc, step, (zero,) * (2 * hp))

    for hh in range(hp):
        ls = slice(hh * 128, (hh + 1) * 128)
        o = _rms(o_s[2 * hh] + o_s[2 * hh + 1], gain_ref[...]) * _silu(z_ref[0, :, ls].astype(F32))
        o_ref[0, :, ls] = o.astype(o_ref.dtype)


def _gdn(qkv, z, ba, conv_w, alog_row, dtb_row, gain, ctx):
    b, s, _ = qkv.shape
    h = GDN_HEADS
    c = GDN_CHUNK
    assert s % (GDN_GROUP * c) == 0
    nc = s // c
    gr = GDN_GROUP * c
    blk = lambda n: jnp.arange(n) // c
    bd_ones = (blk(gr)[:, None] == blk(gr)[None, :]).astype(BF16)
    same = blk(gr)[:, None] == blk(gr)[None, :]
    tri_lo = (same & (jnp.arange(gr)[:, None] >= jnp.arange(gr)[None, :])).astype(BF16)
    tri_up = (same & (jnp.arange(gr)[:, None] <= jnp.arange(gr)[None, :])).astype(BF16)
    hp = GDN_HEADS_PER_STEP
    lanes = hp * 128
    col = lambda off: pl.BlockSpec((1, s, lanes), lambda i, j: (i, 0, off // hp + j))
    cw = lambda off: pl.BlockSpec((SHORT_CONV, lanes), lambda i, j: (0, off // hp + j))
    ch = 2 * hp
    scratch = [pltpu.VMEM((s + 3 * GDN_ZERO_ROWS, 128), F32), pltpu.VMEM((4, s, 128), F32),
               pltpu.VMEM((s, 128), F32), pltpu.VMEM((s, 128), F32), pltpu.VMEM((s, 128), F32),
               pltpu.VMEM((ch, nc, c, GDN_DK + GDN_DV), BF16), pltpu.VMEM((ch, nc, c, GDN_DK), BF16),
               pltpu.VMEM((ch, nc, GDN_DK + c, c), BF16), pltpu.VMEM((ch, nc, 8, 128), F32),
               pltpu.VMEM((2, ch, GDN_DK + c, GDN_DV), BF16), pltpu.VMEM((2, ch, GDN_DK, GDN_DV), F32),
               pltpu.VMEM((2, ch, c, GDN_DV), F32),
               pltpu.VMEM((ch, s, 128), F32)]
    return pl.pallas_call(
        functools.partial(_gdn_kernel, ctx=ctx, seq=s, hp=hp),
        grid=(b, h // hp),
        in_specs=[col(0), col(h), col(2 * h),
                  pl.BlockSpec((1, s, lanes), lambda i, j: (i, 0, j)),
                  pl.BlockSpec((1, s, 128), lambda i, j: (i, 0, 0)),
                  cw(0), cw(h), cw(2 * h),
                  _const_spec((1, 128)), _const_spec((1, 128)), _const_spec((1, 128)),
                  _const_spec(bd_ones.shape), _const_spec(tri_lo.shape), _const_spec(tri_up.shape)],
        out_specs=pl.BlockSpec((1, s, lanes), lambda i, j: (i, 0, j)),
        out_shape=jax.ShapeDtypeStruct((b, s, h * GDN_DV), BF16),
        scratch_shapes=scratch,
        compiler_params=_params(("parallel", "parallel")),
        name="gdn",
    )(qkv, qkv, qkv, z, ba, conv_w, conv_w, conv_w, alog_row, dtb_row, gain.reshape(1, -1), bd_ones, tri_lo, tri_up)


def _rope_tables(n_lat, ctx):
    t = jnp.arange(n_lat)
    row = (t // GRID_W).astype(F32)
    col = (t % GRID_W).astype(F32)
    quarter = MLA_ROPE // 4
    inv_freq = ROPE_BASE ** (-jnp.arange(quarter, dtype=F32) / quarter)
    ang_r = row[:, None] * inv_freq
    ang_c = col[:, None] * inv_freq
    ang = jnp.concatenate([ang_r, ang_r, ang_c, ang_c], axis=-1)
    cos = jnp.concatenate([jnp.ones((ctx, MLA_ROPE), F32), jnp.cos(ang)], axis=0)
    sin = jnp.concatenate([jnp.zeros((ctx, MLA_ROPE), F32), jnp.sin(ang)], axis=0)
    s = ctx + n_lat
    one = jnp.ones((s, MLA_NOPE), F32)
    zn = jnp.zeros((s, MLA_NOPE), F32)
    zp = jnp.zeros((s, MLA_HEAD_PAD - MLA_NOPE - MLA_ROPE), F32)
    cosq = jnp.concatenate([one, cos, zp], axis=1)
    sinq = jnp.concatenate([zn, sin, zp], axis=1)
    zk = jnp.zeros((s, 128 - MLA_ROPE), F32)
    cosk = jnp.concatenate([cos, zk], axis=1)
    sink = jnp.concatenate([sin, zk], axis=1)
    return cosq, sinq, cosk, sink


def _rot_cols(w):
    q = MLA_ROPE // 4
    a, b, c, d = w[..., 0:q], w[..., q:2 * q], w[..., 2 * q:3 * q], w[..., 3 * q:4 * q]
    return jnp.concatenate([-b, a, -d, c], axis=-1)


def _even_weights(w_in, w_uq, w_ukv, w_out):
    d = w_in.shape[0]
    h = GDN_HEADS
    qkv_w = 2 * h * GDN_DK + h * GDN_DV
    vw = h * GDN_DV
    o = 0
    w_qkv = w_in[:, o:o + qkv_w]; o += qkv_w
    w_z = w_in[:, o:o + vw]; o += vw
    w_ba = w_in[:, o:o + 4 * h]; o += 4 * h
    w_cq = w_in[:, o:o + MLA_Q_RANK]; o += MLA_Q_RANK
    w_ckv = w_in[:, o:o + MLA_KV_RANK]; o += MLA_KV_RANK
    w_kr = w_in[:, o:o + MLA_ROPE]
    z64 = jnp.zeros((d, 128 - MLA_ROPE), w_in.dtype)
    w_ba = jnp.concatenate([w_ba, jnp.zeros((d, 128 - 4 * h), w_in.dtype)], axis=1)
    w_mla = jnp.concatenate([w_cq, w_ckv, w_kr, z64, _rot_cols(w_kr), z64], axis=1)
    proj = [w_qkv.astype(BF16), w_z.astype(BF16), w_ba.astype(BF16), w_mla.astype(BF16)]

    rq = MLA_Q_RANK
    wq = w_uq.reshape(rq, MLA_HEADS, MLA_NOPE + MLA_ROPE)
    zpad = jnp.zeros((rq, MLA_HEADS, MLA_HEAD_PAD - MLA_NOPE - MLA_ROPE), w_uq.dtype)
    znope = jnp.zeros((rq, MLA_HEADS, MLA_NOPE), w_uq.dtype)
    wqa = jnp.concatenate([wq, zpad], axis=-1).reshape(rq, -1)
    wqb = jnp.concatenate([znope, _rot_cols(wq[..., MLA_NOPE:]), zpad], axis=-1).reshape(rq, -1)
    rk = MLA_KV_RANK
    wkv = w_ukv.reshape(rk, MLA_HEADS, MLA_NOPE + MLA_V)
    wk = jnp.concatenate([wkv[..., :MLA_NOPE], jnp.zeros((rk, MLA_HEADS, MLA_HEAD_PAD - MLA_NOPE), w_ukv.dtype)],
                         axis=-1).reshape(rk, -1)
    wv = wkv[..., MLA_NOPE:].reshape(rk, -1)
    eye = jnp.eye(MLA_ROPE, dtype=F32)
    blk = jnp.concatenate([jnp.zeros((MLA_ROPE, MLA_NOPE), F32), eye,
                           jnp.zeros((MLA_ROPE, MLA_HEAD_PAD - MLA_NOPE - MLA_ROPE), F32)], axis=1)
    pk = jnp.concatenate([jnp.tile(blk, (1, MLA_HEADS)),
                          jnp.zeros((128 - MLA_ROPE, MLA_HEADS * MLA_HEAD_PAD), F32)], axis=0)
    up = [wqa.astype(BF16), wqb.astype(BF16), wk.astype(BF16), wv.astype(BF16), pk.astype(BF16)]
    out = [w_out[:vw].astype(BF16), w_out[vw:].astype(BF16)]
    return proj, up, out


def _gate_rows(a_log, dt_bias):
    n = 2 * GDN_HEADS
    z = jnp.zeros((n,), F32)
    tail = jnp.zeros((128 - 2 * n,), F32)
    alog_row = jnp.concatenate([z, a_log.astype(F32).reshape(n), tail]).reshape(1, 128)
    dtb_row = jnp.concatenate([z, dt_bias.astype(F32).reshape(n), tail]).reshape(1, 128)
    return alog_row, dtb_row


def kernel(x, c, ctx, c_ctx, ada_w, ada_b, norm_mix_g, norm_ffn_g, ev_w_in, ev_conv_qkv, ev_a_log, ev_dt_bias,
           ev_gdn_norm, ev_q_norm, ev_kv_norm, ev_w_uq, ev_w_ukv, ev_w_out, od_w_qkv, od_rpb, od_w_out,
           ffn_w_in, ffn_conv, ffn_conv_b, ffn_w_out, final_g):
    b, n_lat, d = x.shape
    n_ctx = ctx.shape[1]
    depth = ada_w.shape[0]
    tm = n_ctx
    assert n_lat % tm == 0 and n_lat % GRID_W == 0 and tm % GDN_CHUNK == 0
    rows = n_lat // GRID_W
    assert rows >= NA_KH

    r_pad = -(b + 1) % 8
    cond = jnp.concatenate([c, c_ctx[None, :], jnp.zeros((r_pad, d), c.dtype)], axis=0)
    mods = _ada_table(cond, ada_w, ada_b).reshape(depth, b + 1 + r_pad, 6, d)
    tabs = _rope_tables(n_lat, n_ctx)
    d_ff = ffn_w_out.shape[1]

    xa = jnp.concatenate([ctx, x], axis=1)
    for i in range(depth):
        j = i // 2
        m = mods[i]
        if i % 2 == 0:
            proj_w, up_w, out_w = _even_weights(ev_w_in[j], ev_w_uq[j], ev_w_ukv[j], ev_w_out[j])
            qkv, z, ba, q, k, v = _even_projections(xa, m, norm_mix_g[i], proj_w, tabs, ev_q_norm[j],
                                                    ev_kv_norm[j], up_w, tm)
            alog_row, dtb_row = _gate_rows(ev_a_log[j], ev_dt_bias[j])
            y_a = _gdn(qkv, z, ba, ev_conv_qkv[j], alog_row, dtb_row, ev_gdn_norm[j], n_ctx)
            y_b = _mla_attention(q, k, v, n_ctx, tm)
            ys, w_outs = [y_a, y_b], out_w
        else:
            w = od_w_qkv[j]
            nw = w.shape[1] // 3
            ws = [(w[:, :nw] * (NA_DH ** -0.5)).astype(BF16), w[:, nw:2 * nw].astype(BF16),
                  w[:, 2 * nw:].astype(BF16)]
            q, k, v = _norm_mod_matmul(xa, m, norm_mix_g[i], ws, [BF16, BF16, BF16], 0, tm)
            ys, w_outs = [_na_attention(q, k, v, _na_bias_table(od_rpb[j], rows), n_ctx)], [od_w_out[j].astype(BF16)]
        wi = ffn_w_in[i]
        xa = _mix_ffn_residual(xa, m, ys, w_outs, norm_ffn_g[i], wi[:, :d_ff].astype(BF16),
                               wi[:, d_ff:].astype(BF16), ffn_conv[i], ffn_conv_b[i], ffn_w_out[i].astype(BF16), tm,
                               final_g=final_g if i == depth - 1 else None)
    return xa
```

```python
import functools
import math

import jax
import jax.numpy as jnp
from jax import lax
from jax.experimental import pallas as pl
from jax.experimental.pallas import tpu as pltpu

EPS = 1e-6
ROPE_BASE = 10000.0
GRID_W = 64

GDN_HEADS = 4
GDN_DK = 128
GDN_DV = 128
GDN_CHUNK = 64
GDN_GROUP = 4
GDN_HEADS_PER_STEP = 2
GDN_ZERO_ROWS = 8
GDN_SCAN_UNROLL = 4
SHORT_CONV = 5

MLA_HEADS = 4
MLA_NOPE = 128
MLA_ROPE = 64
MLA_V = 128
MLA_Q_RANK = 256
MLA_KV_RANK = 256
MLA_HEAD_PAD = 256

NA_HEADS = 16
NA_DH = 64
NA_KH = 8
NA_KW = 16
NA_PATTERNS = 8
NA_UNROLL = 5

FFN_CONV = 3
HALO = 16

VMEM_LIMIT = 56 * 1024 * 1024
NEG_BIG = -1e30

BF16 = jnp.bfloat16
F32 = jnp.float32


def _dot(a, b):
    return jnp.dot(a, b, preferred_element_type=F32)


def _dot_nt(a, b):
    return lax.dot_general(a, b, (((1,), (1,)), ((), ())), preferred_element_type=F32)


def _silu(x):
    return x * jax.nn.sigmoid(x)


def _softplus(x):
    return jnp.maximum(x, 0.0) + jnp.log(1.0 + jnp.exp(-jnp.abs(x)))


def _rms(x, g):
    return x * lax.rsqrt(jnp.mean(x * x, axis=-1, keepdims=True) + EPS) * g


def _params(sem):
    return pltpu.CompilerParams(dimension_semantics=sem, vmem_limit_bytes=VMEM_LIMIT)


def _const_spec(shape):
    nd = len(shape)
    return pl.BlockSpec(shape, lambda *_: (0,) * nd)


def _ada_kernel(c_ref, w_ref, b_ref, o_ref):
    s = _silu(c_ref[...]).astype(BF16)
    o_ref[0] = _dot(s, w_ref[0].astype(BF16)) + b_ref[0]


def _ada_table(cond, ada_w, ada_b):
    depth, d, n = ada_w.shape
    r = cond.shape[0]
    tn = 1536
    return pl.pallas_call(
        _ada_kernel,
        grid=(depth, n // tn),
        in_specs=[pl.BlockSpec((r, d), lambda i, j: (0, 0)),
                  pl.BlockSpec((1, d, tn), lambda i, j: (i, 0, j)),
                  pl.BlockSpec((1, 1, tn), lambda i, j: (i, 0, j))],
        out_specs=pl.BlockSpec((1, r, tn), lambda i, j: (i, 0, j)),
        out_shape=jax.ShapeDtypeStruct((depth, r, n), F32),
        compiler_params=_params(("parallel", "parallel")),
        name="ada_table",
    )(cond, ada_w, ada_b.reshape(depth, 1, n))


def _nmm_kernel(x_ref, mod_ref, g_ref, *rest, n_out, shift_idx):
    w_refs, o_refs = rest[:n_out], rest[n_out:]
    shift = mod_ref[0, shift_idx:shift_idx + 1, :]
    scale = mod_ref[0, shift_idx + 1:shift_idx + 2, :]
    h = _rms(x_ref[0], g_ref[...]) * (1.0 + scale) + shift
    hb = h.astype(BF16)
    for w_ref, o_ref in zip(w_refs, o_refs):
        o_ref[0] = _dot(hb, w_ref[...]).astype(o_ref.dtype)


def _norm_mod_matmul(xa, mods, g, ws, out_dtypes, shift_idx, tm):
    b, s, d = xa.shape
    n_out = len(ws)
    in_specs = [pl.BlockSpec((1, tm, d), lambda i, t: (i, t, 0)),
                pl.BlockSpec((1, 6, d), lambda i, t: (jnp.where(t == 0, b, i), 0, 0)),
                _const_spec((1, d))]
    in_specs += [_const_spec(w.shape) for w in ws]
    out_specs = [pl.BlockSpec((1, tm, w.shape[1]), lambda i, t: (i, t, 0)) for w in ws]
    out_shape = [jax.ShapeDtypeStruct((b, s, w.shape[1]), dt) for w, dt in zip(ws, out_dtypes)]
    return pl.pallas_call(
        functools.partial(_nmm_kernel, n_out=n_out, shift_idx=shift_idx),
        grid=(b, s // tm),
        in_specs=in_specs, out_specs=out_specs, out_shape=out_shape,
        compiler_params=_params(("parallel", "parallel")),
        name="norm_mod_matmul",
    )(xa, mods, g.reshape(1, d), *ws)


def _mix_ffn_kernel(x_ref, xp_ref, xn_ref, *rest, n_in, tm, n_tiles, t_off, final):
    y_refs = rest[:3 * n_in]
    w_refs = rest[3 * n_in:4 * n_in]
    mod_ref, g_ref, wu_ref, wg_ref, cw_ref, cb_ref, wo_ref, fg_ref, o_ref = rest[4 * n_in:]
    t = pl.program_id(1) + t_off
    ext = lambda main, prev, nxt: jnp.concatenate([prev[0], main[0], nxt[0]], axis=0)
    proj = _dot(ext(*y_refs[0:3]), w_refs[0][...])
    for j in range(1, n_in):
        proj = proj + _dot(ext(*y_refs[3 * j:3 * j + 3]), w_refs[j][...])
    x_ext = ext(x_ref, xp_ref, xn_ref) + mod_ref[0, 2:3, :] * proj
    x = x_ext[HALO:HALO + tm]
    shift = mod_ref[0, 3:4, :]
    scale = mod_ref[0, 4:5, :]
    gate = mod_ref[0, 5:6, :]
    h_ext = (_rms(x_ext, g_ref[...]) * (1.0 + scale) + shift).astype(BF16)
    gt = _dot(h_ext, wg_ref[...])
    u = _dot(h_ext[HALO:HALO + tm], wu_ref[...])
    prev_ok = (t >= 2).astype(F32)
    next_ok = jnp.logical_and(t >= 1, t < n_tiles - 1).astype(F32)
    row = lax.broadcasted_iota(jnp.int32, (tm + 2 * HALO, 1), 0)
    keep = jnp.where(row < HALO, prev_ok, jnp.where(row >= tm + HALO, next_ok, 1.0))
    gt = gt * keep
    g_prev = pltpu.roll(gt, 1, axis=0)[HALO:HALO + tm]
    g_next = pltpu.roll(gt, tm + 2 * HALO - 1, axis=0)[HALO:HALO + tm]
    g_mid = gt[HALO:HALO + tm]
    conv = g_prev * cw_ref[0:1, :] + g_mid * cw_ref[1:2, :] + g_next * cw_ref[2:3, :] + cb_ref[...]
    act = (_silu(conv) * u).astype(BF16)
    y = x + gate * _dot(act, wo_ref[...])
    o_ref[0] = _rms(y, fg_ref[...]) if final else y


def _mix_ffn_residual(xa, mods, ys, w_outs, g, wu, wg, conv_w, conv_b, wo, tm, final_g=None):
    b, s, d = xa.shape
    f = wu.shape[1]
    n_tiles = s // tm
    hb = tm // HALO
    last = s // HALO - 1
    final = final_g is not None
    t_off = 1 if final else 0
    fg = final_g if final else g
    mod_row = (lambda i, t: (i, 0, 0)) if final else (lambda i, t: (jnp.where(t == 0, b, i), 0, 0))
    main = lambda w: pl.BlockSpec((1, tm, w), lambda i, t: (i, t + t_off, 0))
    prev = lambda w: pl.BlockSpec((1, HALO, w), lambda i, t: (i, jnp.maximum((t + t_off) * hb - 1, 0), 0))
    nxt = lambda w: pl.BlockSpec((1, HALO, w), lambda i, t: (i, jnp.minimum((t + t_off + 1) * hb, last), 0))
    in_specs = [main(d), prev(d), nxt(d)]
    args = [xa, xa, xa]
    for y in ys:
        in_specs += [main(y.shape[2]), prev(y.shape[2]), nxt(y.shape[2])]
        args += [y, y, y]
    in_specs += [_const_spec(w.shape) for w in w_outs]
    in_specs += [pl.BlockSpec((1, 6, d), mod_row), _const_spec((1, d)), _const_spec((d, f)), _const_spec((d, f)),
                 _const_spec((FFN_CONV, f)), _const_spec((1, f)), _const_spec((f, d)), _const_spec((1, d))]
    args += list(w_outs) + [mods, g.reshape(1, d), wu, wg, conv_w, conv_b.reshape(1, f), wo, fg.reshape(1, d)]
    return pl.pallas_call(
        functools.partial(_mix_ffn_kernel, n_in=len(ys), tm=tm, n_tiles=n_tiles, t_off=t_off, final=final),
        grid=(b, n_tiles - t_off),
        in_specs=in_specs,
        out_specs=pl.BlockSpec((1, tm, d), lambda i, t: (i, t, 0)),
        out_shape=jax.ShapeDtypeStruct((b, s - t_off * tm, d), F32),
        compiler_params=_params(("parallel", "parallel")),
        name="mix_ffn",
    )(*args)


def _even_proj_kernel(x_ref, mod_ref, g_ref, wqkv_ref, wz_ref, wba_ref, wmla_ref,
                      cq_ref, sq_ref, ck_ref, sk_ref, qn_ref, kvn_ref, wqa_ref, wqb_ref, wk_ref, wv_ref, pk_ref,
                      qkv_ref, z_ref, ba_ref, q_ref, k_ref, v_ref, *, scale):
    h = _rms(x_ref[0], g_ref[...]) * (1.0 + mod_ref[0, 1:2, :]) + mod_ref[0, 0:1, :]
    hb = h.astype(BF16)
    qkv_ref[0] = _dot(hb, wqkv_ref[...]).astype(qkv_ref.dtype)
    z_ref[0] = _dot(hb, wz_ref[...]).astype(z_ref.dtype)
    ba_ref[0] = _dot(hb, wba_ref[...])
    p = _dot(hb, wmla_ref[...])
    r = MLA_Q_RANK
    cq = _rms(p[:, 0:r], qn_ref[...]).astype(BF16)
    ckv = _rms(p[:, r:2 * r], kvn_ref[...]).astype(BF16)
    k_rope = (p[:, 2 * r:2 * r + 128] * ck_ref[...] + p[:, 2 * r + 128:2 * r + 256] * sk_ref[...]).astype(BF16)
    k_ref[0] = (_dot(ckv, wk_ref[...]) + _dot(k_rope, pk_ref[...])).astype(BF16)
    v_ref[0] = _dot(ckv, wv_ref[...]).astype(BF16)
    qa = _dot(cq, wqa_ref[...])
    qb = _dot(cq, wqb_ref[...])
    cq_t = cq_ref[...] * scale
    sq_t = sq_ref[...] * scale
    for hd in range(MLA_HEADS):
        sl = slice(hd * MLA_HEAD_PAD, (hd + 1) * MLA_HEAD_PAD)
        q_ref[0, :, sl] = (qa[:, sl] * cq_t + qb[:, sl] * sq_t).astype(BF16)


def _even_projections(xa, mods, g, proj_w, tabs, q_norm, kv_norm, up_w, tm):
    b, s, d = xa.shape
    cosq, sinq, cosk, sink = tabs
    scale = (MLA_NOPE + MLA_ROPE) ** -0.5
    hw = MLA_HEADS * MLA_HEAD_PAD
    widths = [proj_w[0].shape[1], proj_w[1].shape[1], proj_w[2].shape[1], hw, hw, MLA_HEADS * MLA_V]
    dtypes = [BF16, BF16, F32, BF16, BF16, BF16]
    row = lambda w: pl.BlockSpec((tm, w), lambda i, t: (t, 0))
    in_specs = [pl.BlockSpec((1, tm, d), lambda i, t: (i, t, 0)),
                pl.BlockSpec((1, 6, d), lambda i, t: (jnp.where(t == 0, b, i), 0, 0)),
                _const_spec((1, d))]
    in_specs += [_const_spec(w.shape) for w in proj_w]
    in_specs += [row(MLA_HEAD_PAD), row(MLA_HEAD_PAD), row(128), row(128),
                 _const_spec((1, MLA_Q_RANK)), _const_spec((1, MLA_KV_RANK))]
    in_specs += [_const_spec(w.shape) for w in up_w]
    return pl.pallas_call(
        functools.partial(_even_proj_kernel, scale=scale),
        grid=(b, s // tm),
        in_specs=in_specs,
        out_specs=[pl.BlockSpec((1, tm, w), lambda i, t: (i, t, 0)) for w in widths],
        out_shape=[jax.ShapeDtypeStruct((b, s, w), dt) for w, dt in zip(widths, dtypes)],
        compiler_params=_params(("parallel", "parallel")),
        name="even_projections",
    )(xa, mods, g.reshape(1, d), *proj_w, cosq, sinq, cosk, sink, q_norm.reshape(1, -1), kv_norm.reshape(1, -1),
      *up_w)


def _softmax_pv(s, v):
    m = jnp.max(s, axis=-1, keepdims=True)
    p = jnp.exp(s - m)
    l = jnp.sum(p, axis=-1, keepdims=True)
    return _dot(p.astype(BF16), v) / l


def _mla_attn_kernel(q_ref, k_ref, v_ref, o_ref, *, ctx):
    t = pl.program_id(2)
    q = q_ref[0]

    @pl.when(t == 0)
    def _():
        o_ref[0] = _softmax_pv(_dot_nt(q, k_ref[0, 0:ctx, :]), v_ref[0, 0:ctx, :]).astype(o_ref.dtype)

    @pl.when(t > 0)
    def _():
        o_ref[0] = _softmax_pv(_dot_nt(q, k_ref[0]), v_ref[0]).astype(o_ref.dtype)


def _mla_attention(q, k, v, ctx, tq):
    b, s, _ = q.shape
    return pl.pallas_call(
        functools.partial(_mla_attn_kernel, ctx=ctx),
        grid=(b, MLA_HEADS, s // tq),
        in_specs=[pl.BlockSpec((1, tq, MLA_HEAD_PAD), lambda i, h, t: (i, t, h)),
                  pl.BlockSpec((1, s, MLA_HEAD_PAD), lambda i, h, t: (i, 0, h)),
                  pl.BlockSpec((1, s, MLA_V), lambda i, h, t: (i, 0, h))],
        out_specs=pl.BlockSpec((1, tq, MLA_V), lambda i, h, t: (i, t, h)),
        out_shape=jax.ShapeDtypeStruct((b, s, MLA_HEADS * MLA_V), BF16),
        compiler_params=_params(("parallel", "parallel", "parallel")),
        name="mla_attention",
    )(q, k, v)


def _na_kernel(q_ref, k_ref, v_ref, bias_ref, o_ref, s0, s1, p0, p1, l0, l1, *, ctx, rows):
    w = GRID_W
    win = NA_KH * w
    s_scr, p_scr, l_scr = (s0, s1), (p0, p1), (l0, l1)
    lane = lax.broadcasted_iota(jnp.int32, (w, 2 * NA_DH), 1)
    first = lane < NA_DH
    kc = k_ref[0, 0:ctx, :]
    vc = v_ref[0, 0:ctx, :]

    def split_heads(qr, first_mask):
        zero = jnp.zeros_like(qr)
        return jnp.concatenate([jnp.where(first_mask, qr, zero), jnp.where(first_mask, zero, qr)], axis=0)

    def merge_heads(o2):
        n = o2.shape[0] // 2
        lane_n = lax.broadcasted_iota(jnp.int32, (n, 2 * NA_DH), 1)
        return jnp.where(lane_n < NA_DH, o2[:n], o2[n:])

    lane_c = lax.broadcasted_iota(jnp.int32, (ctx, 2 * NA_DH), 1)
    qc2 = split_heads(q_ref[0, 0:ctx, :], lane_c < NA_DH)
    o_ref[0, 0:ctx, :] = merge_heads(_softmax_pv(_dot_nt(qc2, kc), vc)).astype(o_ref.dtype)

    def window_start(r):
        return pl.multiple_of(ctx + jnp.clip(r - NA_KH // 2, 0, rows - NA_KH) * w, w)

    def scores(r, slot):
        pat = jnp.where(r < NA_KH // 2, r,
                        jnp.where(r <= rows - NA_KH // 2, NA_KH // 2, r - (rows - NA_KH)))
        q2 = split_heads(q_ref[0, pl.ds(pl.multiple_of(ctx + r * w, w), w), :], first)
        kw = k_ref[0, pl.ds(window_start(r), win), :]
        s_scr[slot][:, 0:win] = _dot_nt(q2, kw) + bias_ref[0, pat]
        s_scr[slot][:, win:win + ctx] = _dot_nt(q2, kc)

    def softmax(slot):
        s = s_scr[slot][...]
        m = jnp.max(s, axis=-1, keepdims=True)
        p = jnp.exp(s - m)
        l_scr[slot][...] = jnp.broadcast_to(jnp.sum(p, axis=-1, keepdims=True), (2 * w, 2 * NA_DH))
        p_scr[slot][...] = p.astype(BF16)

    def values(r, slot):
        vw = v_ref[0, pl.ds(window_start(r), win), :]
        o2 = (_dot(p_scr[slot][:, 0:win], vw) + _dot(p_scr[slot][:, win:win + ctx], vc)) / l_scr[slot][...]
        o_ref[0, pl.ds(pl.multiple_of(ctx + r * w, w), w), :] = merge_heads(o2).astype(o_ref.dtype)

    scores(0, 0)
    scores(1, 1)
    softmax(0)

    def two_rows(u, carry):
        t = 2 * u
        scores(t, 0)
        softmax(1)
        values(t - 2, 0)
        scores(t + 1, 1)
        softmax(0)
        values(t - 1, 1)
        return carry

    lax.fori_loop(1, rows // 2, two_rows, 0, unroll=NA_UNROLL)
    softmax(1)
    values(rows - 2, 0)
    values(rows - 1, 1)


def _na_attention(q, k, v, bias, ctx):
    b, s, width = q.shape
    rows = (s - ctx) // GRID_W
    assert rows % 2 == 0
    pairs = width // (2 * NA_DH)
    keys = NA_KH * GRID_W + ctx
    blk = pl.BlockSpec((1, s, 2 * NA_DH), lambda h, i: (i, 0, h))
    scratch = ([pltpu.VMEM((2 * GRID_W, keys), F32)] * 2 + [pltpu.VMEM((2 * GRID_W, keys), BF16)] * 2
               + [pltpu.VMEM((2 * GRID_W, 2 * NA_DH), F32)] * 2)
    return pl.pallas_call(
        functools.partial(_na_kernel, ctx=ctx, rows=rows),
        grid=(pairs, b),
        in_specs=[blk, blk, blk,
                  pl.BlockSpec((1, NA_PATTERNS, 2 * GRID_W, NA_KH * GRID_W), lambda h, i: (h, 0, 0, 0))],
        out_specs=blk,
        out_shape=jax.ShapeDtypeStruct((b, s, width), BF16),
        scratch_shapes=scratch,
        compiler_params=_params(("parallel", "parallel")),
        name="na_attention",
    )(q, k, v, bias)


def _na_bias_kernel(rpb_ref, o_ref, *, rows):
    w, kh, kw = GRID_W, NA_KH, NA_KW
    pair = pl.program_id(0)
    qi = lax.broadcasted_iota(jnp.int32, (w, w), 0)
    ki = lax.broadcasted_iota(jnp.int32, (w, w), 1)
    col_idx = jnp.clip(ki - qi, -(kw - 1), kw - 1) + kw - 1
    c_start = jnp.clip(qi - kw // 2, 0, w - kw)
    masked = jnp.where((ki >= c_start) & (ki < c_start + kw), 0.0, NEG_BIG)
    rep = (0, 1, 2, 3, 4, rows - 3, rows - 2, rows - 1)
    for hh in range(2):
        head = 2 * pair + hh
        tiles = []
        for dr in range(2 * kh - 1):
            t = masked
            for dc in range(2 * kw - 1):
                t = jnp.where(col_idx == dc, masked + rpb_ref[head, dr, dc], t)
            tiles.append(t)
        for p, r in enumerate(rep):
            r_start = min(max(r - kh // 2, 0), rows - kh)
            for j in range(kh):
                o_ref[0, p, hh * w:(hh + 1) * w, j * w:(j + 1) * w] = tiles[r_start + j - r + kh - 1]


def _na_bias_table(rpb, rows):
    h = rpb.shape[0]
    shape = (h // 2, NA_PATTERNS, 2 * GRID_W, NA_KH * GRID_W)
    return pl.pallas_call(
        functools.partial(_na_bias_kernel, rows=rows),
        grid=(h // 2,),
        in_specs=[pl.BlockSpec(memory_space=pltpu.SMEM)],
        out_specs=pl.BlockSpec((1,) + shape[1:], lambda i: (i, 0, 0, 0)),
        out_shape=jax.ShapeDtypeStruct(shape, F32),
        compiler_params=_params(("parallel",)),
        name="na_bias_table",
    )(rpb.astype(F32))


def _gdn_kernel(q_ref, k_ref, v_ref, z_ref, ba_ref, cwq_ref, cwk_ref, cwv_ref, alog_ref, dtb_ref, gain_ref,
                bd_ref, tril_ref, triu_ref, o_ref, xpad, gate_s, qs, ks, vs, wu_s, qd_s, kq_s, gl_s, bq_s, a_s, c_s, o_s,
                *, ctx, seq, hp):
    c = GDN_CHUNK
    dk = GDN_DK
    grp = GDN_GROUP
    gr = grp * c
    nc = seq // c
    ncc = ctx // c
    ng = seq // gr

    pad = SHORT_CONV // 2
    zp = GDN_ZERO_ROWS

    def short_conv(x, cw):
        zeros = jnp.zeros((zp, 128), F32)
        xpad[0:zp] = zeros
        xpad[zp:zp + ctx] = x[0:ctx]
        xpad[zp + ctx:2 * zp + ctx] = zeros
        xpad[2 * zp + ctx:2 * zp + seq] = x[ctx:seq]
        xpad[2 * zp + seq:3 * zp + seq] = zeros

        def segment(base, n):
            acc = xpad[base - pad:base - pad + n] * cw[0:1, :]
            for j in range(1, SHORT_CONV):
                acc = acc + xpad[base - pad + j:base - pad + j + n] * cw[j:j + 1, :]
            return acc

        return _silu(jnp.concatenate([segment(zp, ctx), segment(2 * zp + ctx, seq - ctx)], axis=0))

    def l2n(x):
        return x * lax.rsqrt(jnp.sum(x * x, axis=-1, keepdims=True) + EPS)

    w8 = 2 * gr
    sup = max(m for m in (1, 2, 3) if ng % m == 0)
    wide = sup * w8
    ii = lax.broadcasted_iota(jnp.int32, (c, wide), 0)
    lane8 = lax.broadcasted_iota(jnp.int32, (c, wide), 1)
    jj = lane8 & (c - 1)
    fwd8 = ((lane8 >> 6) & 1) == 0
    lchunk = lax.broadcasted_iota(jnp.int32, (c, w8), 1) >> 7
    incl = (fwd8 & (ii >= jj)) | (jnp.logical_not(fwd8) & (ii <= jj))
    strict = incl & (ii != jj)
    eye8 = (ii == jj).astype(F32)
    same8 = (ii >> 3) == (jj >> 3)
    lane_g = lax.broadcasted_iota(jnp.int32, (gr, 128), 1)
    fwd_half = lane_g < c
    fwd_half_c = lax.broadcasted_iota(jnp.int32, (c, 128), 1) < c

    def bd(y):
        yb = y.astype(BF16)
        return [jnp.concatenate([yb[:, h * gr:(h + 1) * gr]] * grp, axis=0) * bd_ref[...]
                for h in range(wide // gr)]

    def mm(x, ybd):
        xb = x.astype(BF16)
        return jnp.concatenate([_dot(xb[:, h * gr:(h + 1) * gr], ybd[h]) for h in range(wide // gr)], axis=1)

    def diag8(f):
        out = f[0:c]
        for b in range(1, grp):
            out = jnp.where(lchunk == b, f[b * c:(b + 1) * c], out)
        return out

    def inverse8(a):
        p = jnp.where(same8, -a, 0.0)
        t = eye8 + p
        p2 = mm(p, bd(p))
        r = mm(jnp.concatenate([t, p2], axis=0), bd(p2))
        t = t + r[0:c]
        t = t + mm(t, bd(r[c:2 * c]))
        for sh in (3, 4, 5):
            off = ((ii >> (sh + 1)) == (jj >> (sh + 1))) & ((ii >> sh) != (jj >> sh))
            l_mat = jnp.where(off, a, 0.0)
            t = t - mm(t, bd(mm(l_mat, bd(t))))
        return t

    def chunk_sums(x):
        hi = x.astype(BF16)
        r1 = x - hi.astype(F32)
        mid = r1.astype(BF16)
        lo = (r1 - mid.astype(F32)).astype(BF16)
        parts = jnp.concatenate([hi, mid, lo], axis=1)
        pre = _dot(tril_ref[...], parts)
        suf = _dot(triu_ref[...], parts)
        fold = lambda r: r[:, 0:128] + r[:, 128:256] + r[:, 256:384]
        return fold(pre), fold(suf)

    def pick(x, lane_id):
        col = jnp.sum(jnp.where(lane_g == lane_id, x, 0.0), axis=-1, keepdims=True)
        return jnp.broadcast_to(col, (gr, 128))

    def chunks(x):
        return [x[b * c:(b + 1) * c] for b in range(grp)]

    def across(xf, xb):
        both = jnp.where(fwd_half, xf, xb)
        return jnp.concatenate(chunks(both), axis=1)

    def interleave_rows(xf, xb):
        return jnp.concatenate([s for pair in zip(chunks(xf), chunks(xb)) for s in pair], axis=0)

    def group_inputs(gi, head):
        r0 = pl.multiple_of(gi * gr, gr)
        q = qs[pl.ds(r0, gr), :]
        k = ks[pl.ds(r0, gr), :]
        v = vs[pl.ds(r0, gr), :]
        beta_all = gate_s[0, pl.ds(r0, gr), :]
        pre_all = gate_s[1, pl.ds(r0, gr), :]
        suf_all = gate_s[2, pl.ds(r0, gr), :]
        g_all = gate_s[3, pl.ds(r0, gr), :]
        lane_f = 2 * GDN_HEADS + head
        lane_b = 3 * GDN_HEADS + head
        beta = (pick(beta_all, head), pick(beta_all, GDN_HEADS + head))
        gc = (pick(pre_all, lane_f), pick(suf_all, lane_b))
        grest = (pick(suf_all - g_all, lane_f), pick(pre_all - g_all, lane_b))
        kb = k.astype(BF16)
        kdup = interleave_rows(kb, kb)
        r = _dot_nt(jnp.concatenate([q.astype(BF16), kb], axis=0), kdup)
        g_j = interleave_rows(gc[0], gc[1]).T[0:c]
        return dict(q=q, k=k, v=v, beta=beta, gc=gc, grest=grest, qk8=diag8(r[0:gr]), kk8=diag8(r[gr:2 * gr]),
                    diff=across(gc[0], gc[1]) - g_j, beta8=across(beta[0], beta[1]))

    def prep(si, carry, head, hh):
        groups = [group_inputs(si * sup + g, head) for g in range(sup)]
        cat = lambda name: jnp.concatenate([g[name] for g in groups], axis=1)
        decay = jnp.where(incl, jnp.exp(jnp.minimum(cat("diff"), 0.0)), 0.0)
        t_all = inverse8(jnp.where(strict, cat("beta8") * cat("kk8") * decay, 0.0))
        qkd_all = cat("qk8") * decay
        for gidx, grp_in in enumerate(groups):
            q, k, v, beta, gc, grest = (grp_in[name] for name in ("q", "k", "v", "beta", "gc", "grest"))
            t8 = t_all[:, gidx * w8:(gidx + 1) * w8]
            qkd8 = qkd_all[:, gidx * w8:(gidx + 1) * w8]
            rhs, qd, kd, eg = [], [], [], []
            for d in range(2):
                e = jnp.exp(gc[d])
                eg.append(e)
                rhs.append(jnp.concatenate([(k * (beta[d] * e)).astype(BF16), (v * beta[d]).astype(BF16)], axis=1))
                qd.append((q * e).astype(BF16))
                kd.append(k * jnp.exp(grest[d]))
            for b in range(grp):
                n = (si * sup + gidx) * grp + b
                rows = slice(b * c, (b + 1) * c)
                tb = t8[:, 2 * c * b:2 * c * (b + 1)]
                lhs = jnp.concatenate([jnp.where(fwd_half_c, tb, 0.0), jnp.where(fwd_half_c, 0.0, tb)], axis=0)
                sol = _dot(lhs.astype(BF16), jnp.concatenate([rhs[0][rows], rhs[1][rows]], axis=0))
                for d in range(2):
                    wu_s[2 * hh + d, n] = sol[d * c:(d + 1) * c].astype(BF16)
                    qd_s[2 * hh + d, n] = qd[d][rows]
                    lo = 2 * c * b + c * d
                    kq_s[2 * hh + d, n] = jnp.concatenate([kd[d][rows].T, qkd8[:, lo:lo + c]], axis=0).astype(BF16)
                    edge = b * c + c - 1 if d == 0 else b * c
                    gl_s[2 * hh + d, n] = jnp.broadcast_to(eg[d][edge:edge + 1, :], (8, 128))
        return carry

    def gates(gi, carry):
        r0 = pl.multiple_of(gi * gr, gr)
        ba = ba_ref[0, pl.ds(r0, gr), :]
        g_all = -jnp.exp(alog_ref[...]) * _softplus(ba + dtb_ref[...])
        pre_all, suf_all = chunk_sums(g_all)
        gate_s[0, pl.ds(r0, gr), :] = jax.nn.sigmoid(ba)
        gate_s[1, pl.ds(r0, gr), :] = pre_all
        gate_s[2, pl.ds(r0, gr), :] = suf_all
        gate_s[3, pl.ds(r0, gr), :] = g_all
        return carry

    lax.fori_loop(0, ng, gates, 0, unroll=3)

    for hh in range(hp):
        head = pl.program_id(1) * hp + hh
        ls = slice(hh * 128, (hh + 1) * 128)
        qs[...] = l2n(short_conv(q_ref[0, :, ls].astype(F32), cwq_ref[:, ls])) * (dk ** -0.5)
        ks[...] = l2n(short_conv(k_ref[0, :, ls].astype(F32), cwk_ref[:, ls]))
        vs[...] = short_conv(v_ref[0, :, ls].astype(F32), cwv_ref[:, ls])
        lax.fori_loop(0, ng // sup, functools.partial(prep, head=head, hh=hh), 0)

    def chunk_of(i, ch):
        return i if ch % 2 == 0 else jnp.where(i < ncc, ncc - 1 - i, nc - 1 - (i - ncc))

    def prepare(i, slot):
        for ch in range(2 * hp):
            n = chunk_of(i, ch)
            r3 = _dot(kq_s[ch, n], wu_s[ch, n])
            q_eff = qd_s[ch, n].astype(F32) - r3[dk:dk + c, 0:dk]
            bq_s[slot, ch] = jnp.concatenate([r3[0:dk, 0:dk], q_eff], axis=0).astype(BF16)
            a_s[slot, ch] = r3[0:dk, dk:]
            c_s[slot, ch] = r3[dk:dk + c, dk:]

    prepare(0, 0)

    def step(i, states):
        slot = i % 2
        new = []
        for ch in range(2 * hp):
            n = chunk_of(i, ch)
            state = states[ch]
            r = _dot(bq_s[slot, ch], state.astype(BF16))
            o_s[ch, pl.ds(pl.multiple_of(n * c, c), c), :] = r[dk:dk + c] + c_s[slot, ch]
            new.append(gl_s[ch, n][0:1, :] * state + a_s[slot, ch] - r[0:dk])
        prepare(jnp.minimum(i + 1, nc - 1), 1 - slot)
        return tuple(new)

    zero = jnp.zeros((dk, GDN_DV), F32)
    lax.fori_loop(0, nc, step, (zero,) * (2 * hp), unroll=GDN_SCAN_UNROLL)

    for hh in range(hp):
        ls = slice(hh * 128, (hh + 1) * 128)
        o = _rms(o_s[2 * hh] + o_s[2 * hh + 1], gain_ref[...]) * _silu(z_ref[0, :, ls].astype(F32))
        o_ref[0, :, ls] = o.astype(o_ref.dtype)


def _gdn(qkv, z, ba, conv_w, alog_row, dtb_row, gain, ctx):
    b, s, _ = qkv.shape
    h = GDN_HEADS
    c = GDN_CHUNK
    assert s % (GDN_GROUP * c) == 0
    nc = s // c
    gr = GDN_GROUP * c
    blk = lambda n: jnp.arange(n) // c
    bd_ones = (blk(gr)[:, None] == blk(gr)[None, :]).astype(BF16)
    same = blk(gr)[:, None] == blk(gr)[None, :]
    tri_lo = (same & (jnp.arange(gr)[:, None] >= jnp.arange(gr)[None, :])).astype(BF16)
    tri_up = (same & (jnp.arange(gr)[:, None] <= jnp.arange(gr)[None, :])).astype(BF16)
    hp = GDN_HEADS_PER_STEP
    lanes = hp * 128
    col = lambda off: pl.BlockSpec((1, s, lanes), lambda i, j: (i, 0, off // hp + j))
    cw = lambda off: pl.BlockSpec((SHORT_CONV, lanes), lambda i, j: (0, off // hp + j))
    ch = 2 * hp
    scratch = [pltpu.VMEM((s + 3 * GDN_ZERO_ROWS, 128), F32), pltpu.VMEM((4, s, 128), F32),
               pltpu.VMEM((s, 128), F32), pltpu.VMEM((s, 128), F32), pltpu.VMEM((s, 128), F32),
               pltpu.VMEM((ch, nc, c, GDN_DK + GDN_DV), BF16), pltpu.VMEM((ch, nc, c, GDN_DK), BF16),
               pltpu.VMEM((ch, nc, GDN_DK + c, c), BF16), pltpu.VMEM((ch, nc, 8, 128), F32),
               pltpu.VMEM((2, ch, GDN_DK + c, GDN_DV), BF16), pltpu.VMEM((2, ch, GDN_DK, GDN_DV), F32),
               pltpu.VMEM((2, ch, c, GDN_DV), F32),
               pltpu.VMEM((ch, s, 128), F32)]
    return pl.pallas_call(
        functools.partial(_gdn_kernel, ctx=ctx, seq=s, hp=hp),
        grid=(b, h // hp),
        in_specs=[col(0), col(h), col(2 * h),
                  pl.BlockSpec((1, s, lanes), lambda i, j: (i, 0, j)),
                  pl.BlockSpec((1, s, 128), lambda i, j: (i, 0, 0)),
                  cw(0), cw(h), cw(2 * h),
                  _const_spec((1, 128)), _const_spec((1, 128)), _const_spec((1, 128)),
                  _const_spec(bd_ones.shape), _const_spec(tri_lo.shape), _const_spec(tri_up.shape)],
        out_specs=pl.BlockSpec((1, s, lanes), lambda i, j: (i, 0, j)),
        out_shape=jax.ShapeDtypeStruct((b, s, h * GDN_DV), BF16),
        scratch_shapes=scratch,
        compiler_params=_params(("parallel", "parallel")),
        name="gdn",
    )(qkv, qkv, qkv, z, ba, conv_w, conv_w, conv_w, alog_row, dtb_row, gain.reshape(1, -1), bd_ones, tri_lo, tri_up)


def _rope_tables(n_lat, ctx):
    t = jnp.arange(n_lat)
    row = (t // GRID_W).astype(F32)
    col = (t % GRID_W).astype(F32)
    quarter = MLA_ROPE // 4
    inv_freq = ROPE_BASE ** (-jnp.arange(quarter, dtype=F32) / quarter)
    ang_r = row[:, None] * inv_freq
    ang_c = col[:, None] * inv_freq
    ang = jnp.concatenate([ang_r, ang_r, ang_c, ang_c], axis=-1)
    cos = jnp.concatenate([jnp.ones((ctx, MLA_ROPE), F32), jnp.cos(ang)], axis=0)
    sin = jnp.concatenate([jnp.zeros((ctx, MLA_ROPE), F32), jnp.sin(ang)], axis=0)
    s = ctx + n_lat
    one = jnp.ones((s, MLA_NOPE), F32)
    zn = jnp.zeros((s, MLA_NOPE), F32)
    zp = jnp.zeros((s, MLA_HEAD_PAD - MLA_NOPE - MLA_ROPE), F32)
    cosq = jnp.concatenate([one, cos, zp], axis=1)
    sinq = jnp.concatenate([zn, sin, zp], axis=1)
    zk = jnp.zeros((s, 128 - MLA_ROPE), F32)
    cosk = jnp.concatenate([cos, zk], axis=1)
    sink = jnp.concatenate([sin, zk], axis=1)
    return cosq, sinq, cosk, sink


def _rot_cols(w):
    q = MLA_ROPE // 4
    a, b, c, d = w[..., 0:q], w[..., q:2 * q], w[..., 2 * q:3 * q], w[..., 3 * q:4 * q]
    return jnp.concatenate([-b, a, -d, c], axis=-1)


def _even_weights(w_in, w_uq, w_ukv, w_out):
    d = w_in.shape[0]
    h = GDN_HEADS
    qkv_w = 2 * h * GDN_DK + h * GDN_DV
    vw = h * GDN_DV
    o = 0
    w_qkv = w_in[:, o:o + qkv_w]; o += qkv_w
    w_z = w_in[:, o:o + vw]; o += vw
    w_ba = w_in[:, o:o + 4 * h]; o += 4 * h
    w_cq = w_in[:, o:o + MLA_Q_RANK]; o += MLA_Q_RANK
    w_ckv = w_in[:, o:o + MLA_KV_RANK]; o += MLA_KV_RANK
    w_kr = w_in[:, o:o + MLA_ROPE]
    z64 = jnp.zeros((d, 128 - MLA_ROPE), w_in.dtype)
    w_ba = jnp.concatenate([w_ba, jnp.zeros((d, 128 - 4 * h), w_in.dtype)], axis=1)
    w_mla = jnp.concatenate([w_cq, w_ckv, w_kr, z64, _rot_cols(w_kr), z64], axis=1)
    proj = [w_qkv.astype(BF16), w_z.astype(BF16), w_ba.astype(BF16), w_mla.astype(BF16)]

    rq = MLA_Q_RANK
    wq = w_uq.reshape(rq, MLA_HEADS, MLA_NOPE + MLA_ROPE)
    zpad = jnp.zeros((rq, MLA_HEADS, MLA_HEAD_PAD - MLA_NOPE - MLA_ROPE), w_uq.dtype)
    znope = jnp.zeros((rq, MLA_HEADS, MLA_NOPE), w_uq.dtype)
    wqa = jnp.concatenate([wq, zpad], axis=-1).reshape(rq, -1)
    wqb = jnp.concatenate([znope, _rot_cols(wq[..., MLA_NOPE:]), zpad], axis=-1).reshape(rq, -1)
    rk = MLA_KV_RANK
    wkv = w_ukv.reshape(rk, MLA_HEADS, MLA_NOPE + MLA_V)
    wk = jnp.concatenate([wkv[..., :MLA_NOPE], jnp.zeros((rk, MLA_HEADS, MLA_HEAD_PAD - MLA_NOPE), w_ukv.dtype)],
                         axis=-1).reshape(rk, -1)
    wv = wkv[..., MLA_NOPE:].reshape(rk, -1)
    eye = jnp.eye(MLA_ROPE, dtype=F32)
    blk = jnp.concatenate([jnp.zeros((MLA_ROPE, MLA_NOPE), F32), eye,
                           jnp.zeros((MLA_ROPE, MLA_HEAD_PAD - MLA_NOPE - MLA_ROPE), F32)], axis=1)
    pk = jnp.concatenate([jnp.tile(blk, (1, MLA_HEADS)),
                          jnp.zeros((128 - MLA_ROPE, MLA_HEADS * MLA_HEAD_PAD), F32)], axis=0)
    up = [wqa.astype(BF16), wqb.astype(BF16), wk.astype(BF16), wv.astype(BF16), pk.astype(BF16)]
    out = [w_out[:vw].astype(BF16), w_out[vw:].astype(BF16)]
    return proj, up, out


def _gate_rows(a_log, dt_bias):
    n = 2 * GDN_HEADS
    z = jnp.zeros((n,), F32)
    tail = jnp.zeros((128 - 2 * n,), F32)
    alog_row = jnp.concatenate([z, a_log.astype(F32).reshape(n), tail]).reshape(1, 128)
    dtb_row = jnp.concatenate([z, dt_bias.astype(F32).reshape(n), tail]).reshape(1, 128)
    return alog_row, dtb_row


def kernel(x, c, ctx, c_ctx, ada_w, ada_b, norm_mix_g, norm_ffn_g, ev_w_in, ev_conv_qkv, ev_a_log, ev_dt_bias,
           ev_gdn_norm, ev_q_norm, ev_kv_norm, ev_w_uq, ev_w_ukv, ev_w_out, od_w_qkv, od_rpb, od_w_out,
           ffn_w_in, ffn_conv, ffn_conv_b, ffn_w_out, final_g):
    b, n_lat, d = x.shape
    n_ctx = ctx.shape[1]
    depth = ada_w.shape[0]
    tm = n_ctx
    assert n_lat % tm == 0 and n_lat % GRID_W == 0 and tm % GDN_CHUNK == 0
    rows = n_lat // GRID_W
    assert rows >= NA_KH

    r_pad = -(b + 1) % 8
    cond = jnp.concatenate([c, c_ctx[None, :], jnp.zeros((r_pad, d), c.dtype)], axis=0)
    mods = _ada_table(cond, ada_w, ada_b).reshape(depth, b + 1 + r_pad, 6, d)
    tabs = _rope_tables(n_lat, n_ctx)
    d_ff = ffn_w_out.shape[1]

    xa = jnp.concatenate([ctx, x], axis=1)
    for i in range(depth):
        j = i // 2
        m = mods[i]
        if i % 2 == 0:
            proj_w, up_w, out_w = _even_weights(ev_w_in[j], ev_w_uq[j], ev_w_ukv[j], ev_w_out[j])
            qkv, z, ba, q, k, v = _even_projections(xa, m, norm_mix_g[i], proj_w, tabs, ev_q_norm[j],
                                                    ev_kv_norm[j], up_w, tm)
            alog_row, dtb_row = _gate_rows(ev_a_log[j], ev_dt_bias[j])
            y_a = _gdn(qkv, z, ba, ev_conv_qkv[j], alog_row, dtb_row, ev_gdn_norm[j], n_ctx)
            y_b = _mla_attention(q, k, v, n_ctx, tm)
            ys, w_outs = [y_a, y_b], out_w
        else:
            w = od_w_qkv[j]
            nw = w.shape[1] // 3
            ws = [(w[:, :nw] * (NA_DH ** -0.5)).astype(BF16), w[:, nw:2 * nw].astype(BF16),
                  w[:, 2 * nw:].astype(BF16)]
            q, k, v = _norm_mod_matmul(xa, m, norm_mix_g[i], ws, [BF16, BF16, BF16], 0, tm)
            ys, w_outs = [_na_attention(q, k, v, _na_bias_table(od_rpb[j], rows), n_ctx)], [od_w_out[j].astype(BF16)]
        wi = ffn_w_in[i]
        xa = _mix_ffn_residual(xa, m, ys, w_outs, norm_ffn_g[i], wi[:, :d_ff].astype(BF16),
                               wi[:, d_ff:].astype(BF16), ffn_conv[i], ffn_conv_b[i], ffn_w_out[i].astype(BF16), tm,
                               final_g=final_g if i == depth - 1 else None)
    return xa
```

```python
import functools
import math

import jax
import jax.numpy as jnp
from jax import lax
from jax.experimental import pallas as pl
from jax.experimental.pallas import tpu as pltpu

EPS = 1e-6
ROPE_BASE = 10000.0
GRID_W = 64

GDN_HEADS = 4
GDN_DK = 128
GDN_DV = 128
GDN_CHUNK = 64
GDN_GROUP = 4
GDN_HEADS_PER_STEP = 2
GDN_ZERO_ROWS = 8
GDN_SCAN_UNROLL = 4
SHORT_CONV = 5

MLA_HEADS = 4
MLA_NOPE = 128
MLA_ROPE = 64
MLA_V = 128
MLA_Q_RANK = 256
MLA_KV_RANK = 256
MLA_HEAD_PAD = 256

NA_HEADS = 16
NA_DH = 64
NA_KH = 8
NA_KW = 16
NA_PATTERNS = 8
NA_UNROLL = 15

FFN_CONV = 3
HALO = 16

VMEM_LIMIT = 56 * 1024 * 1024
NEG_BIG = -1e30

BF16 = jnp.bfloat16
F32 = jnp.float32


def _dot(a, b):
    return jnp.dot(a, b, preferred_element_type=F32)


def _dot_nt(a, b):
    return lax.dot_general(a, b, (((1,), (1,)), ((), ())), preferred_element_type=F32)


def _silu(x):
    return x * jax.nn.sigmoid(x)


def _softplus(x):
    return jnp.maximum(x, 0.0) + jnp.log(1.0 + jnp.exp(-jnp.abs(x)))


def _rms(x, g):
    return x * lax.rsqrt(jnp.mean(x * x, axis=-1, keepdims=True) + EPS) * g


def _params(sem):
    return pltpu.CompilerParams(dimension_semantics=sem, vmem_limit_bytes=VMEM_LIMIT)


def _const_spec(shape):
    nd = len(shape)
    return pl.BlockSpec(shape, lambda *_: (0,) * nd)


def _ada_kernel(c_ref, w_ref, b_ref, o_ref):
    s = _silu(c_ref[...]).astype(BF16)
    o_ref[0] = _dot(s, w_ref[0].astype(BF16)) + b_ref[0]


def _ada_table(cond, ada_w, ada_b):
    depth, d, n = ada_w.shape
    r = cond.shape[0]
    tn = 1536
    return pl.pallas_call(
        _ada_kernel,
        grid=(depth, n // tn),
        in_specs=[pl.BlockSpec((r, d), lambda i, j: (0, 0)),
                  pl.BlockSpec((1, d, tn), lambda i, j: (i, 0, j)),
                  pl.BlockSpec((1, 1, tn), lambda i, j: (i, 0, j))],
        out_specs=pl.BlockSpec((1, r, tn), lambda i, j: (i, 0, j)),
        out_shape=jax.ShapeDtypeStruct((depth, r, n), F32),
        compiler_params=_params(("parallel", "parallel")),
        name="ada_table",
    )(cond, ada_w, ada_b.reshape(depth, 1, n))


def _nmm_kernel(x_ref, mod_ref, g_ref, *rest, n_out, shift_idx):
    w_refs, o_refs = rest[:n_out], rest[n_out:]
    shift = mod_ref[0, shift_idx:shift_idx + 1, :]
    scale = mod_ref[0, shift_idx + 1:shift_idx + 2, :]
    h = _rms(x_ref[0], g_ref[...]) * (1.0 + scale) + shift
    hb = h.astype(BF16)
    for w_ref, o_ref in zip(w_refs, o_refs):
        o_ref[0] = _dot(hb, w_ref[...]).astype(o_ref.dtype)


def _norm_mod_matmul(xa, mods, g, ws, out_dtypes, shift_idx, tm):
    b, s, d = xa.shape
    n_out = len(ws)
    in_specs = [pl.BlockSpec((1, tm, d), lambda i, t: (i, t, 0)),
                pl.BlockSpec((1, 6, d), lambda i, t: (jnp.where(t == 0, b, i), 0, 0)),
                _const_spec((1, d))]
    in_specs += [_const_spec(w.shape) for w in ws]
    out_specs = [pl.BlockSpec((1, tm, w.shape[1]), lambda i, t: (i, t, 0)) for w in ws]
    out_shape = [jax.ShapeDtypeStruct((b, s, w.shape[1]), dt) for w, dt in zip(ws, out_dtypes)]
    return pl.pallas_call(
        functools.partial(_nmm_kernel, n_out=n_out, shift_idx=shift_idx),
        grid=(b, s // tm),
        in_specs=in_specs, out_specs=out_specs, out_shape=out_shape,
        compiler_params=_params(("parallel", "parallel")),
        name="norm_mod_matmul",
    )(xa, mods, g.reshape(1, d), *ws)


def _mix_ffn_kernel(x_ref, xp_ref, xn_ref, *rest, n_in, tm, n_tiles, t_off, final):
    y_refs = rest[:3 * n_in]
    w_refs = rest[3 * n_in:4 * n_in]
    mod_ref, g_ref, wu_ref, wg_ref, cw_ref, cb_ref, wo_ref, fg_ref, o_ref = rest[4 * n_in:]
    t = pl.program_id(1) + t_off
    ext = lambda main, prev, nxt: jnp.concatenate([prev[0], main[0], nxt[0]], axis=0)
    proj = _dot(ext(*y_refs[0:3]), w_refs[0][...])
    for j in range(1, n_in):
        proj = proj + _dot(ext(*y_refs[3 * j:3 * j + 3]), w_refs[j][...])
    x_ext = ext(x_ref, xp_ref, xn_ref) + mod_ref[0, 2:3, :] * proj
    x = x_ext[HALO:HALO + tm]
    shift = mod_ref[0, 3:4, :]
    scale = mod_ref[0, 4:5, :]
    gate = mod_ref[0, 5:6, :]
    h_ext = (_rms(x_ext, g_ref[...]) * (1.0 + scale) + shift).astype(BF16)
    gt = _dot(h_ext, wg_ref[...])
    u = _dot(h_ext[HALO:HALO + tm], wu_ref[...])
    prev_ok = (t >= 2).astype(F32)
    next_ok = jnp.logical_and(t >= 1, t < n_tiles - 1).astype(F32)
    row = lax.broadcasted_iota(jnp.int32, (tm + 2 * HALO, 1), 0)
    keep = jnp.where(row < HALO, prev_ok, jnp.where(row >= tm + HALO, next_ok, 1.0))
    gt = gt * keep
    g_prev = pltpu.roll(gt, 1, axis=0)[HALO:HALO + tm]
    g_next = pltpu.roll(gt, tm + 2 * HALO - 1, axis=0)[HALO:HALO + tm]
    g_mid = gt[HALO:HALO + tm]
    conv = g_prev * cw_ref[0:1, :] + g_mid * cw_ref[1:2, :] + g_next * cw_ref[2:3, :] + cb_ref[...]
    act = (_silu(conv) * u).astype(BF16)
    y = x + gate * _dot(act, wo_ref[...])
    o_ref[0] = _rms(y, fg_ref[...]) if final else y


def _mix_ffn_residual(xa, mods, ys, w_outs, g, wu, wg, conv_w, conv_b, wo, tm, final_g=None):
    b, s, d = xa.shape
    f = wu.shape[1]
    n_tiles = s // tm
    hb = tm // HALO
    last = s // HALO - 1
    final = final_g is not None
    t_off = 1 if final else 0
    fg = final_g if final else g
    mod_row = (lambda i, t: (i, 0, 0)) if final else (lambda i, t: (jnp.where(t == 0, b, i), 0, 0))
    main = lambda w: pl.BlockSpec((1, tm, w), lambda i, t: (i, t + t_off, 0))
    prev = lambda w: pl.BlockSpec((1, HALO, w), lambda i, t: (i, jnp.maximum((t + t_off) * hb - 1, 0), 0))
    nxt = lambda w: pl.BlockSpec((1, HALO, w), lambda i, t: (i, jnp.minimum((t + t_off + 1) * hb, last), 0))
    in_specs = [main(d), prev(d), nxt(d)]
    args = [xa, xa, xa]
    for y in ys:
        in_specs += [main(y.shape[2]), prev(y.shape[2]), nxt(y.shape[2])]
        args += [y, y, y]
    in_specs += [_const_spec(w.shape) for w in w_outs]
    in_specs += [pl.BlockSpec((1, 6, d), mod_row), _const_spec((1, d)), _const_spec((d, f)), _const_spec((d, f)),
                 _const_spec((FFN_CONV, f)), _const_spec((1, f)), _const_spec((f, d)), _const_spec((1, d))]
    args += list(w_outs) + [mods, g.reshape(1, d), wu, wg, conv_w, conv_b.reshape(1, f), wo, fg.reshape(1, d)]
    return pl.pallas_call(
        functools.partial(_mix_ffn_kernel, n_in=len(ys), tm=tm, n_tiles=n_tiles, t_off=t_off, final=final),
        grid=(b, n_tiles - t_off),
        in_specs=in_specs,
        out_specs=pl.BlockSpec((1, tm, d), lambda i, t: (i, t, 0)),
        out_shape=jax.ShapeDtypeStruct((b, s - t_off * tm, d), F32),
        compiler_params=_params(("parallel", "parallel")),
        name="mix_ffn",
    )(*args)


def _even_proj_kernel(x_ref, mod_ref, g_ref, wqkv_ref, wz_ref, wba_ref, wmla_ref,
                      cq_ref, sq_ref, ck_ref, sk_ref, qn_ref, kvn_ref, wqa_ref, wqb_ref, wk_ref, wv_ref, pk_ref,
                      qkv_ref, z_ref, ba_ref, q_ref, k_ref, v_ref, *, scale):
    h = _rms(x_ref[0], g_ref[...]) * (1.0 + mod_ref[0, 1:2, :]) + mod_ref[0, 0:1, :]
    hb = h.astype(BF16)
    qkv_ref[0] = _dot(hb, wqkv_ref[...]).astype(qkv_ref.dtype)
    z_ref[0] = _dot(hb, wz_ref[...]).astype(z_ref.dtype)
    ba_ref[0] = _dot(hb, wba_ref[...])
    p = _dot(hb, wmla_ref[...])
    r = MLA_Q_RANK
    cq = _rms(p[:, 0:r], qn_ref[...]).astype(BF16)
    ckv = _rms(p[:, r:2 * r], kvn_ref[...]).astype(BF16)
    k_rope = (p[:, 2 * r:2 * r + 128] * ck_ref[...] + p[:, 2 * r + 128:2 * r + 256] * sk_ref[...]).astype(BF16)
    k_ref[0] = (_dot(ckv, wk_ref[...]) + _dot(k_rope, pk_ref[...])).astype(BF16)
    v_ref[0] = _dot(ckv, wv_ref[...]).astype(BF16)
    qa = _dot(cq, wqa_ref[...])
    qb = _dot(cq, wqb_ref[...])
    cq_t = cq_ref[...] * scale
    sq_t = sq_ref[...] * scale
    for hd in range(MLA_HEADS):
        sl = slice(hd * MLA_HEAD_PAD, (hd + 1) * MLA_HEAD_PAD)
        q_ref[0, :, sl] = (qa[:, sl] * cq_t + qb[:, sl] * sq_t).astype(BF16)


def _even_projections(xa, mods, g, proj_w, tabs, q_norm, kv_norm, up_w, tm):
    b, s, d = xa.shape
    cosq, sinq, cosk, sink = tabs
    scale = (MLA_NOPE + MLA_ROPE) ** -0.5
    hw = MLA_HEADS * MLA_HEAD_PAD
    widths = [proj_w[0].shape[1], proj_w[1].shape[1], proj_w[2].shape[1], hw, hw, MLA_HEADS * MLA_V]
    dtypes = [BF16, BF16, F32, BF16, BF16, BF16]
    row = lambda w: pl.BlockSpec((tm, w), lambda i, t: (t, 0))
    in_specs = [pl.BlockSpec((1, tm, d), lambda i, t: (i, t, 0)),
                pl.BlockSpec((1, 6, d), lambda i, t: (jnp.where(t == 0, b, i), 0, 0)),
                _const_spec((1, d))]
    in_specs += [_const_spec(w.shape) for w in proj_w]
    in_specs += [row(MLA_HEAD_PAD), row(MLA_HEAD_PAD), row(128), row(128),
                 _const_spec((1, MLA_Q_RANK)), _const_spec((1, MLA_KV_RANK))]
    in_specs += [_const_spec(w.shape) for w in up_w]
    return pl.pallas_call(
        functools.partial(_even_proj_kernel, scale=scale),
        grid=(b, s // tm),
        in_specs=in_specs,
        out_specs=[pl.BlockSpec((1, tm, w), lambda i, t: (i, t, 0)) for w in widths],
        out_shape=[jax.ShapeDtypeStruct((b, s, w), dt) for w, dt in zip(widths, dtypes)],
        compiler_params=_params(("parallel", "parallel")),
        name="even_projections",
    )(xa, mods, g.reshape(1, d), *proj_w, cosq, sinq, cosk, sink, q_norm.reshape(1, -1), kv_norm.reshape(1, -1),
      *up_w)


def _softmax_pv(s, v):
    m = jnp.max(s, axis=-1, keepdims=True)
    p = jnp.exp(s - m)
    l = jnp.sum(p, axis=-1, keepdims=True)
    return _dot(p.astype(BF16), v) / l


def _mla_attn_kernel(q_ref, k_ref, v_ref, o_ref, *, ctx):
    t = pl.program_id(2)
    q = q_ref[0]

    @pl.when(t == 0)
    def _():
        o_ref[0] = _softmax_pv(_dot_nt(q, k_ref[0, 0:ctx, :]), v_ref[0, 0:ctx, :]).astype(o_ref.dtype)

    @pl.when(t > 0)
    def _():
        o_ref[0] = _softmax_pv(_dot_nt(q, k_ref[0]), v_ref[0]).astype(o_ref.dtype)


def _mla_attention(q, k, v, ctx, tq):
    b, s, _ = q.shape
    return pl.pallas_call(
        functools.partial(_mla_attn_kernel, ctx=ctx),
        grid=(b, MLA_HEADS, s // tq),
        in_specs=[pl.BlockSpec((1, tq, MLA_HEAD_PAD), lambda i, h, t: (i, t, h)),
                  pl.BlockSpec((1, s, MLA_HEAD_PAD), lambda i, h, t: (i, 0, h)),
                  pl.BlockSpec((1, s, MLA_V), lambda i, h, t: (i, 0, h))],
        out_specs=pl.BlockSpec((1, tq, MLA_V), lambda i, h, t: (i, t, h)),
        out_shape=jax.ShapeDtypeStruct((b, s, MLA_HEADS * MLA_V), BF16),
        compiler_params=_params(("parallel", "parallel", "parallel")),
        name="mla_attention",
    )(q, k, v)


def _na_kernel(q_ref, k_ref, v_ref, bias_ref, o_ref, s0, s1, p0, p1, l0, l1, *, ctx, rows):
    w = GRID_W
    win = NA_KH * w
    s_scr, p_scr, l_scr = (s0, s1), (p0, p1), (l0, l1)
    lane = lax.broadcasted_iota(jnp.int32, (w, 2 * NA_DH), 1)
    first = lane < NA_DH
    kc = k_ref[0, 0:ctx, :]
    vc = v_ref[0, 0:ctx, :]

    def split_heads(qr, first_mask):
        zero = jnp.zeros_like(qr)
        return jnp.concatenate([jnp.where(first_mask, qr, zero), jnp.where(first_mask, zero, qr)], axis=0)

    def merge_heads(o2):
        n = o2.shape[0] // 2
        lane_n = lax.broadcasted_iota(jnp.int32, (n, 2 * NA_DH), 1)
        return jnp.where(lane_n < NA_DH, o2[:n], o2[n:])

    lane_c = lax.broadcasted_iota(jnp.int32, (ctx, 2 * NA_DH), 1)
    qc2 = split_heads(q_ref[0, 0:ctx, :], lane_c < NA_DH)
    o_ref[0, 0:ctx, :] = merge_heads(_softmax_pv(_dot_nt(qc2, kc), vc)).astype(o_ref.dtype)

    def window_start(r):
        return pl.multiple_of(ctx + jnp.clip(r - NA_KH // 2, 0, rows - NA_KH) * w, w)

    def scores(r, slot):
        pat = jnp.where(r < NA_KH // 2, r,
                        jnp.where(r <= rows - NA_KH // 2, NA_KH // 2, r - (rows - NA_KH)))
        q2 = split_heads(q_ref[0, pl.ds(pl.multiple_of(ctx + r * w, w), w), :], first)
        kw = k_ref[0, pl.ds(window_start(r), win), :]
        s_scr[slot][:, 0:win] = _dot_nt(q2, kw) + bias_ref[0, pat]
        s_scr[slot][:, win:win + ctx] = _dot_nt(q2, kc)

    def softmax(slot):
        s = s_scr[slot][...]
        m = jnp.max(s, axis=-1, keepdims=True)
        p = jnp.exp(s - m)
        l_scr[slot][...] = jnp.broadcast_to(jnp.sum(p, axis=-1, keepdims=True), (2 * w, 2 * NA_DH))
        p_scr[slot][...] = p.astype(BF16)

    def values(r, slot):
        vw = v_ref[0, pl.ds(window_start(r), win), :]
        o2 = (_dot(p_scr[slot][:, 0:win], vw) + _dot(p_scr[slot][:, win:win + ctx], vc)) / l_scr[slot][...]
        o_ref[0, pl.ds(pl.multiple_of(ctx + r * w, w), w), :] = merge_heads(o2).astype(o_ref.dtype)

    scores(0, 0)
    scores(1, 1)
    softmax(0)

    def two_rows(u, carry):
        t = 2 * u
        scores(t, 0)
        softmax(1)
        values(t - 2, 0)
        scores(t + 1, 1)
        softmax(0)
        values(t - 1, 1)
        return carry

    lax.fori_loop(1, rows // 2, two_rows, 0, unroll=NA_UNROLL)
    softmax(1)
    values(rows - 2, 0)
    values(rows - 1, 1)


def _na_attention(q, k, v, bias, ctx):
    b, s, width = q.shape
    rows = (s - ctx) // GRID_W
    assert rows % 2 == 0
    pairs = width // (2 * NA_DH)
    keys = NA_KH * GRID_W + ctx
    blk = pl.BlockSpec((1, s, 2 * NA_DH), lambda h, i: (i, 0, h))
    scratch = ([pltpu.VMEM((2 * GRID_W, keys), F32)] * 2 + [pltpu.VMEM((2 * GRID_W, keys), BF16)] * 2
               + [pltpu.VMEM((2 * GRID_W, 2 * NA_DH), F32)] * 2)
    return pl.pallas_call(
        functools.partial(_na_kernel, ctx=ctx, rows=rows),
        grid=(pairs, b),
        in_specs=[blk, blk, blk,
                  pl.BlockSpec((1, NA_PATTERNS, 2 * GRID_W, NA_KH * GRID_W), lambda h, i: (h, 0, 0, 0))],
        out_specs=blk,
        out_shape=jax.ShapeDtypeStruct((b, s, width), BF16),
        scratch_shapes=scratch,
        compiler_params=_params(("parallel", "parallel")),
        name="na_attention",
    )(q, k, v, bias)


def _na_bias_kernel(rpb_ref, o_ref, *, rows):
    w, kh, kw = GRID_W, NA_KH, NA_KW
    pair = pl.program_id(0)
    qi = lax.broadcasted_iota(jnp.int32, (w, w), 0)
    ki = lax.broadcasted_iota(jnp.int32, (w, w), 1)
    col_idx = jnp.clip(ki - qi, -(kw - 1), kw - 1) + kw - 1
    c_start = jnp.clip(qi - kw // 2, 0, w - kw)
    masked = jnp.where((ki >= c_start) & (ki < c_start + kw), 0.0, NEG_BIG)
    rep = (0, 1, 2, 3, 4, rows - 3, rows - 2, rows - 1)
    for hh in range(2):
        head = 2 * pair + hh
        tiles = []
        for dr in range(2 * kh - 1):
            t = masked
            for dc in range(2 * kw - 1):
                t = jnp.where(col_idx == dc, masked + rpb_ref[head, dr, dc], t)
            tiles.append(t)
        for p, r in enumerate(rep):
            r_start = min(max(r - kh // 2, 0), rows - kh)
            for j in range(kh):
                o_ref[0, p, hh * w:(hh + 1) * w, j * w:(j + 1) * w] = tiles[r_start + j - r + kh - 1]


def _na_bias_table(rpb, rows):
    h = rpb.shape[0]
    shape = (h // 2, NA_PATTERNS, 2 * GRID_W, NA_KH * GRID_W)
    return pl.pallas_call(
        functools.partial(_na_bias_kernel, rows=rows),
        grid=(h // 2,),
        in_specs=[pl.BlockSpec(memory_space=pltpu.SMEM)],
        out_specs=pl.BlockSpec((1,) + shape[1:], lambda i: (i, 0, 0, 0)),
        out_shape=jax.ShapeDtypeStruct(shape, F32),
        compiler_params=_params(("parallel",)),
        name="na_bias_table",
    )(rpb.astype(F32))


def _gdn_kernel(q_ref, k_ref, v_ref, z_ref, ba_ref, cwq_ref, cwk_ref, cwv_ref, alog_ref, dtb_ref, gain_ref,
                bd_ref, tril_ref, triu_ref, o_ref, xpad, gate_s, qs, ks, vs, wu_s, qd_s, kq_s, gl_s, bq_s, a_s, c_s, o_s,
                *, ctx, seq, hp):
    c = GDN_CHUNK
    dk = GDN_DK
    grp = GDN_GROUP
    gr = grp * c
    nc = seq // c
    ncc = ctx // c
    ng = seq // gr

    pad = SHORT_CONV // 2
    zp = GDN_ZERO_ROWS

    def short_conv(x, cw):
        zeros = jnp.zeros((zp, 128), F32)
        xpad[0:zp] = zeros
        xpad[zp:zp + ctx] = x[0:ctx]
        xpad[zp + ctx:2 * zp + ctx] = zeros
        xpad[2 * zp + ctx:2 * zp + seq] = x[ctx:seq]
        xpad[2 * zp + seq:3 * zp + seq] = zeros

        def segment(base, n):
            acc = xpad[base - pad:base - pad + n] * cw[0:1, :]
            for j in range(1, SHORT_CONV):
                acc = acc + xpad[base - pad + j:base - pad + j + n] * cw[j:j + 1, :]
            return acc

        return _silu(jnp.concatenate([segment(zp, ctx), segment(2 * zp + ctx, seq - ctx)], axis=0))

    def l2n(x):
        return x * lax.rsqrt(jnp.sum(x * x, axis=-1, keepdims=True) + EPS)

    w8 = 2 * gr
    sup = max(m for m in (1, 2, 3) if ng % m == 0)
    wide = sup * w8
    ii = lax.broadcasted_iota(jnp.int32, (c, wide), 0)
    lane8 = lax.broadcasted_iota(jnp.int32, (c, wide), 1)
    jj = lane8 & (c - 1)
    fwd8 = ((lane8 >> 6) & 1) == 0
    lchunk = lax.broadcasted_iota(jnp.int32, (c, w8), 1) >> 7
    incl = (fwd8 & (ii >= jj)) | (jnp.logical_not(fwd8) & (ii <= jj))
    strict = incl & (ii != jj)
    eye8 = (ii == jj).astype(F32)
    same8 = (ii >> 3) == (jj >> 3)
    lane_g = lax.broadcasted_iota(jnp.int32, (gr, 128), 1)
    fwd_half = lane_g < c
    fwd_half_c = lax.broadcasted_iota(jnp.int32, (c, 128), 1) < c

    def bd(y):
        yb = y.astype(BF16)
        return [jnp.concatenate([yb[:, h * gr:(h + 1) * gr]] * grp, axis=0) * bd_ref[...]
                for h in range(wide // gr)]

    def mm(x, ybd):
        xb = x.astype(BF16)
        return jnp.concatenate([_dot(xb[:, h * gr:(h + 1) * gr], ybd[h]) for h in range(wide // gr)], axis=1)

    def diag8(f):
        out = f[0:c]
        for b in range(1, grp):
            out = jnp.where(lchunk == b, f[b * c:(b + 1) * c], out)
        return out

    def inverse8(a):
        p = jnp.where(same8, -a, 0.0)
        t = eye8 + p
        p2 = mm(p, bd(p))
        r = mm(jnp.concatenate([t, p2], axis=0), bd(p2))
        t = t + r[0:c]
        t = t + mm(t, bd(r[c:2 * c]))
        for sh in (3, 4, 5):
            off = ((ii >> (sh + 1)) == (jj >> (sh + 1))) & ((ii >> sh) != (jj >> sh))
            l_mat = jnp.where(off, a, 0.0)
            t = t - mm(t, bd(mm(l_mat, bd(t))))
        return t

    def chunk_sums(x):
        hi = x.astype(BF16)
        r1 = x - hi.astype(F32)
        mid = r1.astype(BF16)
        lo = (r1 - mid.astype(F32)).astype(BF16)
        parts = jnp.concatenate([hi, mid, lo], axis=1)
        pre = _dot(tril_ref[...], parts)
        suf = _dot(triu_ref[...], parts)
        fold = lambda r: r[:, 0:128] + r[:, 128:256] + r[:, 256:384]
        return fold(pre), fold(suf)

    def pick(x, lane_id):
        col = jnp.sum(jnp.where(lane_g == lane_id, x, 0.0), axis=-1, keepdims=True)
        return jnp.broadcast_to(col, (gr, 128))

    def chunks(x):
        return [x[b * c:(b + 1) * c] for b in range(grp)]

    def across(xf, xb):
        both = jnp.where(fwd_half, xf, xb)
        return jnp.concatenate(chunks(both), axis=1)

    def interleave_rows(xf, xb):
        return jnp.concatenate([s for pair in zip(chunks(xf), chunks(xb)) for s in pair], axis=0)

    def group_inputs(gi, head):
        r0 = pl.multiple_of(gi * gr, gr)
        q = qs[pl.ds(r0, gr), :]
        k = ks[pl.ds(r0, gr), :]
        v = vs[pl.ds(r0, gr), :]
        beta_all = gate_s[0, pl.ds(r0, gr), :]
        pre_all = gate_s[1, pl.ds(r0, gr), :]
        suf_all = gate_s[2, pl.ds(r0, gr), :]
        g_all = gate_s[3, pl.ds(r0, gr), :]
        lane_f = 2 * GDN_HEADS + head
        lane_b = 3 * GDN_HEADS + head
        beta = (pick(beta_all, head), pick(beta_all, GDN_HEADS + head))
        gc = (pick(pre_all, lane_f), pick(suf_all, lane_b))
        grest = (pick(suf_all - g_all, lane_f), pick(pre_all - g_all, lane_b))
        kb = k.astype(BF16)
        kdup = interleave_rows(kb, kb)
        r = _dot_nt(jnp.concatenate([q.astype(BF16), kb], axis=0), kdup)
        g_j = interleave_rows(gc[0], gc[1]).T[0:c]
        return dict(q=q, k=k, v=v, beta=beta, gc=gc, grest=grest, qk8=diag8(r[0:gr]), kk8=diag8(r[gr:2 * gr]),
                    diff=across(gc[0], gc[1]) - g_j, beta8=across(beta[0], beta[1]))

    def prep(si, carry, head, hh):
        groups = [group_inputs(si * sup + g, head) for g in range(sup)]
        cat = lambda name: jnp.concatenate([g[name] for g in groups], axis=1)
        decay = jnp.where(incl, jnp.exp(jnp.minimum(cat("diff"), 0.0)), 0.0)
        t_all = inverse8(jnp.where(strict, cat("beta8") * cat("kk8") * decay, 0.0))
        qkd_all = cat("qk8") * decay
        for gidx, grp_in in enumerate(groups):
            q, k, v, beta, gc, grest = (grp_in[name] for name in ("q", "k", "v", "beta", "gc", "grest"))
            t8 = t_all[:, gidx * w8:(gidx + 1) * w8]
            qkd8 = qkd_all[:, gidx * w8:(gidx + 1) * w8]
            rhs, qd, kd, eg = [], [], [], []
            for d in range(2):
                e = jnp.exp(gc[d])
                eg.append(e)
                rhs.append(jnp.concatenate([(k * (beta[d] * e)).astype(BF16), (v * beta[d]).astype(BF16)], axis=1))
                qd.append((q * e).astype(BF16))
                kd.append(k * jnp.exp(grest[d]))
            for b in range(grp):
                n = (si * sup + gidx) * grp + b
                rows = slice(b * c, (b + 1) * c)
                tb = t8[:, 2 * c * b:2 * c * (b + 1)]
                lhs = jnp.concatenate([jnp.where(fwd_half_c, tb, 0.0), jnp.where(fwd_half_c, 0.0, tb)], axis=0)
                sol = _dot(lhs.astype(BF16), jnp.concatenate([rhs[0][rows], rhs[1][rows]], axis=0))
                for d in range(2):
                    wu_s[2 * hh + d, n] = sol[d * c:(d + 1) * c].astype(BF16)
                    qd_s[2 * hh + d, n] = qd[d][rows]
                    lo = 2 * c * b + c * d
                    kq_s[2 * hh + d, n] = jnp.concatenate([kd[d][rows].T, qkd8[:, lo:lo + c]], axis=0).astype(BF16)
                    edge = b * c + c - 1 if d == 0 else b * c
                    gl_s[2 * hh + d, n] = jnp.broadcast_to(eg[d][edge:edge + 1, :], (8, 128))
        return carry

    def gates(gi, carry):
        r0 = pl.multiple_of(gi * gr, gr)
        ba = ba_ref[0, pl.ds(r0, gr), :]
        g_all = -jnp.exp(alog_ref[...]) * _softplus(ba + dtb_ref[...])
        pre_all, suf_all = chunk_sums(g_all)
        gate_s[0, pl.ds(r0, gr), :] = jax.nn.sigmoid(ba)
        gate_s[1, pl.ds(r0, gr), :] = pre_all
        gate_s[2, pl.ds(r0, gr), :] = suf_all
        gate_s[3, pl.ds(r0, gr), :] = g_all
        return carry

    lax.fori_loop(0, ng, gates, 0, unroll=3)

    for hh in range(hp):
        head = pl.program_id(1) * hp + hh
        ls = slice(hh * 128, (hh + 1) * 128)
        qs[...] = l2n(short_conv(q_ref[0, :, ls].astype(F32), cwq_ref[:, ls])) * (dk ** -0.5)
        ks[...] = l2n(short_conv(k_ref[0, :, ls].astype(F32), cwk_ref[:, ls]))
        vs[...] = short_conv(v_ref[0, :, ls].astype(F32), cwv_ref[:, ls])
        lax.fori_loop(0, ng // sup, functools.partial(prep, head=head, hh=hh), 0)

    def chunk_of(i, ch):
        return i if ch % 2 == 0 else jnp.where(i < ncc, ncc - 1 - i, nc - 1 - (i - ncc))

    def prepare(i, slot):
        for ch in range(2 * hp):
            n = chunk_of(i, ch)
            r3 = _dot(kq_s[ch, n], wu_s[ch, n])
            q_eff = qd_s[ch, n].astype(F32) - r3[dk:dk + c, 0:dk]
            bq_s[slot, ch] = jnp.concatenate([r3[0:dk, 0:dk], q_eff], axis=0).astype(BF16)
            a_s[slot, ch] = r3[0:dk, dk:]
            c_s[slot, ch] = r3[dk:dk + c, dk:]

    prepare(0, 0)

    def step(i, states):
        slot = i % 2
        new = []
        for ch in range(2 * hp):
            n = chunk_of(i, ch)
            state = states[ch]
            r = _dot(bq_s[slot, ch], state.astype(BF16))
            o_s[ch, pl.ds(pl.multiple_of(n * c, c), c), :] = r[dk:dk + c] + c_s[slot, ch]
            new.append(gl_s[ch, n][0:1, :] * state + a_s[slot, ch] - r[0:dk])
        prepare(jnp.minimum(i + 1, nc - 1), 1 - slot)
        return tuple(new)

    zero = jnp.zeros((dk, GDN_DV), F32)
    lax.fori_loop(0, nc, step, (zero,) * (2 * hp), unroll=GDN_SCAN_UNROLL)

    for hh in range(hp):
        ls = slice(hh * 128, (hh + 1) * 128)
        o = _rms(o_s[2 * hh] + o_s[2 * hh + 1], gain_ref[...]) * _silu(z_ref[0, :, ls].astype(F32))
        o_ref[0, :, ls] = o.astype(o_ref.dtype)


def _gdn(qkv, z, ba, conv_w, alog_row, dtb_row, gain, ctx):
    b, s, _ = qkv.shape
    h = GDN_HEADS
    c = GDN_CHUNK
    assert s % (GDN_GROUP * c) == 0
    nc = s // c
    gr = GDN_GROUP * c
    blk = lambda n: jnp.arange(n) // c
    bd_ones = (blk(gr)[:, None] == blk(gr)[None, :]).astype(BF16)
    same = blk(gr)[:, None] == blk(gr)[None, :]
    tri_lo = (same & (jnp.arange(gr)[:, None] >= jnp.arange(gr)[None, :])).astype(BF16)
    tri_up = (same & (jnp.arange(gr)[:, None] <= jnp.arange(gr)[None, :])).astype(BF16)
    hp = GDN_HEADS_PER_STEP
    lanes = hp * 128
    col = lambda off: pl.BlockSpec((1, s, lanes), lambda i, j: (i, 0, off // hp + j))
    cw = lambda off: pl.BlockSpec((SHORT_CONV, lanes), lambda i, j: (0, off // hp + j))
    ch = 2 * hp
    scratch = [pltpu.VMEM((s + 3 * GDN_ZERO_ROWS, 128), F32), pltpu.VMEM((4, s, 128), F32),
               pltpu.VMEM((s, 128), F32), pltpu.VMEM((s, 128), F32), pltpu.VMEM((s, 128), F32),
               pltpu.VMEM((ch, nc, c, GDN_DK + GDN_DV), BF16), pltpu.VMEM((ch, nc, c, GDN_DK), BF16),
               pltpu.VMEM((ch, nc, GDN_DK + c, c), BF16), pltpu.VMEM((ch, nc, 8, 128), F32),
               pltpu.VMEM((2, ch, GDN_DK + c, GDN_DV), BF16), pltpu.VMEM((2, ch, GDN_DK, GDN_DV), F32),
               pltpu.VMEM((2, ch, c, GDN_DV), F32),
               pltpu.VMEM((ch, s, 128), F32)]
    return pl.pallas_call(
        functools.partial(_gdn_kernel, ctx=ctx, seq=s, hp=hp),
        grid=(b, h // hp),
        in_specs=[col(0), col(h), col(2 * h),
                  pl.BlockSpec((1, s, lanes), lambda i, j: (i, 0, j)),
                  pl.BlockSpec((1, s, 128), lambda i, j: (i, 0, 0)),
                  cw(0), cw(h), cw(2 * h),
                  _const_spec((1, 128)), _const_spec((1, 128)), _const_spec((1, 128)),
                  _const_spec(bd_ones.shape), _const_spec(tri_lo.shape), _const_spec(tri_up.shape)],
        out_specs=pl.BlockSpec((1, s, lanes), lambda i, j: (i, 0, j)),
        out_shape=jax.ShapeDtypeStruct((b, s, h * GDN_DV), BF16),
        scratch_shapes=scratch,
        compiler_params=_params(("parallel", "parallel")),
        name="gdn",
    )(qkv, qkv, qkv, z, ba, conv_w, conv_w, conv_w, alog_row, dtb_row, gain.reshape(1, -1), bd_ones, tri_lo, tri_up)


def _rope_tables(n_lat, ctx):
    t = jnp.arange(n_lat)
    row = (t // GRID_W).astype(F32)
    col = (t % GRID_W).astype(F32)
    quarter = MLA_ROPE // 4
    inv_freq = ROPE_BASE ** (-jnp.arange(quarter, dtype=F32) / quarter)
    ang_r = row[:, None] * inv_freq
    ang_c = col[:, None] * inv_freq
    ang = jnp.concatenate([ang_r, ang_r, ang_c, ang_c], axis=-1)
    cos = jnp.concatenate([jnp.ones((ctx, MLA_ROPE), F32), jnp.cos(ang)], axis=0)
    sin = jnp.concatenate([jnp.zeros((ctx, MLA_ROPE), F32), jnp.sin(ang)], axis=0)
    s = ctx + n_lat
    one = jnp.ones((s, MLA_NOPE), F32)
    zn = jnp.zeros((s, MLA_NOPE), F32)
    zp = jnp.zeros((s, MLA_HEAD_PAD - MLA_NOPE - MLA_ROPE), F32)
    cosq = jnp.concatenate([one, cos, zp], axis=1)
    sinq = jnp.concatenate([zn, sin, zp], axis=1)
    zk = jnp.zeros((s, 128 - MLA_ROPE), F32)
    cosk = jnp.concatenate([cos, zk], axis=1)
    sink = jnp.concatenate([sin, zk], axis=1)
    return cosq, sinq, cosk, sink


def _rot_cols(w):
    q = MLA_ROPE // 4
    a, b, c, d = w[..., 0:q], w[..., q:2 * q], w[..., 2 * q:3 * q], w[..., 3 * q:4 * q]
    return jnp.concatenate([-b, a, -d, c], axis=-1)


def _even_weights(w_in, w_uq, w_ukv, w_out):
    d = w_in.shape[0]
    h = GDN_HEADS
    qkv_w = 2 * h * GDN_DK + h * GDN_DV
    vw = h * GDN_DV
    o = 0
    w_qkv = w_in[:, o:o + qkv_w]; o += qkv_w
    w_z = w_in[:, o:o + vw]; o += vw
    w_ba = w_in[:, o:o + 4 * h]; o += 4 * h
    w_cq = w_in[:, o:o + MLA_Q_RANK]; o += MLA_Q_RANK
    w_ckv = w_in[:, o:o + MLA_KV_RANK]; o += MLA_KV_RANK
    w_kr = w_in[:, o:o + MLA_ROPE]
    z64 = jnp.zeros((d, 128 - MLA_ROPE), w_in.dtype)
    w_ba = jnp.concatenate([w_ba, jnp.zeros((d, 128 - 4 * h), w_in.dtype)], axis=1)
    w_mla = jnp.concatenate([w_cq, w_ckv, w_kr, z64, _rot_cols(w_kr), z64], axis=1)
    proj = [w_qkv.astype(BF16), w_z.astype(BF16), w_ba.astype(BF16), w_mla.astype(BF16)]

    rq = MLA_Q_RANK
    wq = w_uq.reshape(rq, MLA_HEADS, MLA_NOPE + MLA_ROPE)
    zpad = jnp.zeros((rq, MLA_HEADS, MLA_HEAD_PAD - MLA_NOPE - MLA_ROPE), w_uq.dtype)
    znope = jnp.zeros((rq, MLA_HEADS, MLA_NOPE), w_uq.dtype)
    wqa = jnp.concatenate([wq, zpad], axis=-1).reshape(rq, -1)
    wqb = jnp.concatenate([znope, _rot_cols(wq[..., MLA_NOPE:]), zpad], axis=-1).reshape(rq, -1)
    rk = MLA_KV_RANK
    wkv = w_ukv.reshape(rk, MLA_HEADS, MLA_NOPE + MLA_V)
    wk = jnp.concatenate([wkv[..., :MLA_NOPE], jnp.zeros((rk, MLA_HEADS, MLA_HEAD_PAD - MLA_NOPE), w_ukv.dtype)],
                         axis=-1).reshape(rk, -1)
    wv = wkv[..., MLA_NOPE:].reshape(rk, -1)
    eye = jnp.eye(MLA_ROPE, dtype=F32)
    blk = jnp.concatenate([jnp.zeros((MLA_ROPE, MLA_NOPE), F32), eye,
                           jnp.zeros((MLA_ROPE, MLA_HEAD_PAD - MLA_NOPE - MLA_ROPE), F32)], axis=1)
    pk = jnp.concatenate([jnp.tile(blk, (1, MLA_HEADS)),
                          jnp.zeros((128 - MLA_ROPE, MLA_HEADS * MLA_HEAD_PAD), F32)], axis=0)
    up = [wqa.astype(BF16), wqb.astype(BF16), wk.astype(BF16), wv.astype(BF16), pk.astype(BF16)]
    out = [w_out[:vw].astype(BF16), w_out[vw:].astype(BF16)]
    return proj, up, out


def _gate_rows(a_log, dt_bias):
    n = 2 * GDN_HEADS
    z = jnp.zeros((n,), F32)
    tail = jnp.zeros((128 - 2 * n,), F32)
    alog_row = jnp.concatenate([z, a_log.astype(F32).reshape(n), tail]).reshape(1, 128)
    dtb_row = jnp.concatenate([z, dt_bias.astype(F32).reshape(n), tail]).reshape(1, 128)
    return alog_row, dtb_row


def kernel(x, c, ctx, c_ctx, ada_w, ada_b, norm_mix_g, norm_ffn_g, ev_w_in, ev_conv_qkv, ev_a_log, ev_dt_bias,
           ev_gdn_norm, ev_q_norm, ev_kv_norm, ev_w_uq, ev_w_ukv, ev_w_out, od_w_qkv, od_rpb, od_w_out,
           ffn_w_in, ffn_conv, ffn_conv_b, ffn_w_out, final_g):
    b, n_lat, d = x.shape
    n_ctx = ctx.shape[1]
    depth = ada_w.shape[0]
    tm = n_ctx
    assert n_lat % tm == 0 and n_lat % GRID_W == 0 and tm % GDN_CHUNK == 0
    rows = n_lat // GRID_W
    assert rows >= NA_KH

    r_pad = -(b + 1) % 8
    cond = jnp.concatenate([c, c_ctx[None, :], jnp.zeros((r_pad, d), c.dtype)], axis=0)
    mods = _ada_table(cond, ada_w, ada_b).reshape(depth, b + 1 + r_pad, 6, d)
    tabs = _rope_tables(n_lat, n_ctx)
    d_ff = ffn_w_out.shape[1]

    xa = jnp.concatenate([ctx, x], axis=1)
    for i in range(depth):
        j = i // 2
        m = mods[i]
        if i % 2 == 0:
            proj_w, up_w, out_w = _even_weights(ev_w_in[j], ev_w_uq[j], ev_w_ukv[j], ev_w_out[j])
            qkv, z, ba, q, k, v = _even_projections(xa, m, norm_mix_g[i], proj_w, tabs, ev_q_norm[j],
                                                    ev_kv_norm[j], up_w, tm)
            alog_row, dtb_row = _gate_rows(ev_a_log[j], ev_dt_bias[j])
            y_a = _gdn(qkv, z, ba, ev_conv_qkv[j], alog_row, dtb_row, ev_gdn_norm[j], n_ctx)
            y_b = _mla_attention(q, k, v, n_ctx, tm)
            ys, w_outs = [y_a, y_b], out_w
        else:
            w = od_w_qkv[j]
            nw = w.shape[1] // 3
            ws = [(w[:, :nw] * (NA_DH ** -0.5)).astype(BF16), w[:, nw:2 * nw].astype(BF16),
                  w[:, 2 * nw:].astype(BF16)]
            q, k, v = _norm_mod_matmul(xa, m, norm_mix_g[i], ws, [BF16, BF16, BF16], 0, tm)
            ys, w_outs = [_na_attention(q, k, v, _na_bias_table(od_rpb[j], rows), n_ctx)], [od_w_out[j].astype(BF16)]
        wi = ffn_w_in[i]
        xa = _mix_ffn_residual(xa, m, ys, w_outs, norm_ffn_g[i], wi[:, :d_ff].astype(BF16),
                               wi[:, d_ff:].astype(BF16), ffn_conv[i], ffn_conv_b[i], ffn_w_out[i].astype(BF16), tm,
                               final_g=final_g if i == depth - 1 else None)
    return xa
```
